```python
import math
import jax
import jax.numpy as jnp
from jax import lax
import numpy as np

D_MODEL = 1024
BATCH = 4
SEQ = 4096
DEPTH = 4
DEC_BATCH = 128
DEC_SEQ = 1
PAST_LEN = 2048
PAGE_SIZE = 128

N_EVEN = (DEPTH + 1) // 2
N_ODD = DEPTH // 2
GLA_HEADS = 4
GLA_DK = 64
GLA_DV = 128
GLA_KEY_W = GLA_HEADS * GLA_DK
GLA_VAL_W = GLA_HEADS * GLA_DV
GLA_RANK = 16
GLA_TAU = 16.0
GLA_CHUNK = 64
CONV_CH = 512
CONV_WIDTH = 31
MIX_W = GLA_VAL_W + CONV_CH
EVEN_IN_W = 2 * GLA_KEY_W + 2 * GLA_VAL_W + GLA_RANK + 2 * CONV_CH
DIFF_HEADS = 8
DIFF_DH = 64
DIFF_DV = 2 * DIFF_DH
DIFF_QK_W = DIFF_HEADS * 2 * DIFF_DH
DIFF_V_W = DIFF_HEADS * DIFF_DV
ATT_BLOCK = 128
REL_BUCKETS = 32
REL_MAX_DIST = 128
N_GROUPS = 4
EXPERTS_PER_GROUP = 8
N_EXPERTS = N_GROUPS * EXPERTS_PER_GROUP
TOP_K_IN_GROUP = 2
D_FF_EXPERT = 512
MOE_BLOCK = 128
PLE_DIM = 256
NORM_EPS = 1e-6

kernel_name = 'hybrid_gla_conformer_diffattn_hmoe_step'


def rmsnorm(x, g):
    xf = x.astype(jnp.float32)
    y = xf * lax.rsqrt(jnp.mean(xf * xf, axis=-1, keepdims=True) + NORM_EPS)
    return (y * g.astype(jnp.float32)).astype(x.dtype)


def layernorm(x, g, b):
    xf = x.astype(jnp.float32)
    mu = jnp.mean(xf, axis=-1, keepdims=True)
    var = jnp.mean(jnp.square(xf - mu), axis=-1, keepdims=True)
    y = (xf - mu) * lax.rsqrt(var + NORM_EPS) * g.astype(jnp.float32) + b.astype(jnp.float32)
    return y.astype(x.dtype)


def gla_chunk(state, inp):
    q, k, v, g = inp
    b = jnp.cumsum(g, axis=1)
    c = q.shape[1]
    causal = jnp.tril(jnp.ones((c, c), dtype=bool))
    diff = b[:, :, None] - b[:, None, :]
    decay = jnp.exp(jnp.where(causal[None, :, :, None, None], diff, -jnp.inf))
    scores = jnp.sum(q[:, :, None] * k[:, None, :] * decay, axis=-1)
    o = (jnp.einsum('btsh,bshv->bthv', scores, v)
         + jnp.einsum('bthk,bhkv->bthv', q * jnp.exp(b), state))
    b_end = b[:, -1]
    new_state = (jnp.exp(b_end)[..., None] * state
                 + jnp.einsum('bshk,bshv->bhkv', k * jnp.exp(b_end[:, None] - b), v))
    return new_state, o


def gla_recurrence(q, k, v, g, state0):
    bsz, seq = q.shape[:2]
    c = GLA_CHUNK if seq % GLA_CHUNK == 0 else seq
    n = seq // c

    def chunks(t):
        return jnp.moveaxis(t.astype(jnp.float32).reshape((bsz, n, c) + t.shape[2:]), 1, 0)

    state, o = lax.scan(gla_chunk, state0.astype(jnp.float32), (chunks(q), chunks(k), chunks(v), chunks(g)))
    o = jnp.moveaxis(o, 0, 1).reshape((bsz, seq) + o.shape[3:])
    return o, state


def even_mixer(h, gla0, conv0, w_in, w_lr, b_lr, gla_g, conv_w, conv_b, ln_g, ln_b, w_out):
    bsz, seq, _ = h.shape
    z = h @ w_in
    offs = np.cumsum([GLA_KEY_W, GLA_KEY_W, GLA_VAL_W, GLA_VAL_W, GLA_RANK]).tolist()
    q, k, v, r, lr, u = jnp.split(z, offs, axis=-1)
    q = q.reshape(bsz, seq, GLA_HEADS, GLA_DK) * (GLA_DK ** -0.5)
    k = k.reshape(bsz, seq, GLA_HEADS, GLA_DK)
    v = v.reshape(bsz, seq, GLA_HEADS, GLA_DV)
    log_a = jax.nn.log_sigmoid((lr @ w_lr + b_lr).astype(jnp.float32)) / GLA_TAU
    log_a = log_a.reshape(bsz, seq, GLA_HEADS, GLA_DK)
    o, gla_new = gla_recurrence(q, k, v, log_a, gla0)
    o = rmsnorm(o, gla_g).astype(h.dtype).reshape(bsz, seq, GLA_VAL_W) * jax.nn.silu(r)
    a, gate = jnp.split(u, 2, axis=-1)
    u = a * jax.nn.sigmoid(gate)
    u_ext = jnp.concatenate([conv0.astype(u.dtype), u], axis=1)
    c = lax.conv_general_dilated(u_ext, conv_w[:, None, :].astype(u.dtype), window_strides=(1,),
                                 padding='VALID', dimension_numbers=('NWC', 'WIO', 'NWC'),
                                 feature_group_count=CONV_CH) + conv_b
    c = jax.nn.silu(layernorm(c, ln_g, ln_b))
    out = jnp.concatenate([o, c], axis=-1) @ w_out
    return out, gla_new, u_ext[:, -(CONV_WIDTH - 1):]


def lambda_init(layer):
    return 0.8 - 0.6 * math.exp(-0.3 * layer)


def diff_lambda(lq1, lk1, lq2, lk2, lam0):
    f = lambda t: t.astype(jnp.float32)
    return jnp.exp(jnp.sum(f(lq1) * f(lk1))) - jnp.exp(jnp.sum(f(lq2) * f(lk2))) + lam0


def rel_bucket(n):
    max_exact = REL_BUCKETS // 2
    nf = jnp.maximum(n, 1).astype(jnp.float32)
    large = max_exact + (jnp.log(nf / max_exact) / math.log(REL_MAX_DIST / max_exact)
                         * (REL_BUCKETS - max_exact)).astype(jnp.int32)
    return jnp.where(n < max_exact, jnp.maximum(n, 0), jnp.minimum(large, REL_BUCKETS - 1))


def diff_attend(q, k, v, qpos, kpos, rel_bias, lam):
    n = qpos[:, None] - kpos[None, :]
    bias = jnp.moveaxis(rel_bias[rel_bucket(n)], -1, 0).astype(jnp.float32)
    logits = jnp.einsum('bqhmd,bkhmd->bhmqk', q, k).astype(jnp.float32) * (DIFF_DH ** -0.5) + bias[None, :, None]
    logits = jnp.where(n >= 0, logits, -jnp.inf)
    probs = jax.nn.softmax(logits, axis=-1)
    attn = probs[:, :, 0] - lam * probs[:, :, 1]
    return jnp.einsum('bhqk,bkhd->bqhd', attn.astype(v.dtype), v)


def diff_project(h, w_qkv, q_g, k_g):
    bsz, seq, _ = h.shape
    q, k, v = jnp.split(h @ w_qkv, [DIFF_QK_W, 2 * DIFF_QK_W], axis=-1)
    q = rmsnorm(q.reshape(bsz, seq, DIFF_HEADS, 2, DIFF_DH), q_g)
    k = rmsnorm(k.reshape(bsz, seq, DIFF_HEADS, 2, DIFF_DH), k_g)
    v = v.reshape(bsz, seq, DIFF_HEADS, DIFF_DV)
    return q, k, v


def diff_attn_prompt(q, k, v, rel_bias, lam):
    bsz, seq = q.shape[:2]
    nb = seq // ATT_BLOCK
    pos = jnp.arange(seq, dtype=jnp.int32)
    qb = jnp.moveaxis(q.reshape((bsz, nb, ATT_BLOCK) + q.shape[2:]), 1, 0)
    ob = lax.map(lambda a: diff_attend(a[0], k, v, a[1], pos, rel_bias, lam),
                 (qb, pos.reshape(nb, ATT_BLOCK)))
    return jnp.moveaxis(ob, 0, 1).reshape((bsz, seq) + ob.shape[3:])


def diff_attn_sample(q, k_new, v_new, k_past, v_past, rel_bias, lam):
    bsz, seq = q.shape[:2]
    k_past = k_past.reshape((bsz, -1) + k_past.shape[3:])
    v_past = v_past.reshape((bsz, -1) + v_past.shape[3:])
    past = k_past.shape[1]
    k = jnp.concatenate([k_past.astype(k_new.dtype), k_new], axis=1)
    v = jnp.concatenate([v_past.astype(v_new.dtype), v_new], axis=1)
    kpos = jnp.arange(past + seq, dtype=jnp.int32)
    qpos = past + jnp.arange(seq, dtype=jnp.int32)
    return diff_attend(q, k, v, qpos, kpos, rel_bias, lam)


def diff_finish(o, sub_g, lam0, w_out):
    bsz, seq = o.shape[:2]
    o = rmsnorm(o, sub_g) * (1.0 - lam0)
    return o.reshape(bsz, seq, DIFF_V_W) @ w_out


def moe_dispatch(x2, experts, gates, w_gate, w_up, w_down):
    n_tok, k_sel = experts.shape
    n_exp = w_gate.shape[0]
    n_asg = n_tok * k_sel
    flat_e = experts.reshape(n_asg)
    order = jnp.argsort(flat_e)
    e_sorted = flat_e[order]
    tok = (order // k_sel).astype(jnp.int32)
    counts = jnp.zeros((n_exp,), jnp.int32).at[flat_e].add(1)
    starts = jnp.cumsum(counts) - counts
    padded = (counts + MOE_BLOCK - 1) // MOE_BLOCK * MOE_BLOCK
    pad_end = jnp.cumsum(padded)
    dest = (pad_end - padded)[e_sorted] + jnp.arange(n_asg, dtype=jnp.int32) - starts[e_sorted]
    n_blk = (n_asg + n_exp * (MOE_BLOCK - 1) + MOE_BLOCK - 1) // MOE_BLOCK
    row_tok = jnp.zeros((n_blk * MOE_BLOCK,), jnp.int32).at[dest].set(tok)
    blk_e = jnp.minimum(jnp.searchsorted(pad_end, jnp.arange(n_blk, dtype=jnp.int32) * MOE_BLOCK, side='right'),
                        n_exp - 1)
    xb = x2[row_tok].reshape(n_blk, MOE_BLOCK, x2.shape[-1])

    def expert_block(args):
        xs, e = args
        return (jax.nn.silu(xs @ w_gate[e]) * (xs @ w_up[e])) @ w_down[e]

    yb = lax.map(expert_block, (xb, blk_e)).reshape(n_blk * MOE_BLOCK, x2.shape[-1])
    contrib = yb[dest] * gates.reshape(n_asg)[order][:, None]
    return jnp.zeros_like(x2).at[tok].add(contrib)


def hier_moe(h, rg_w, rg_b, re_w, re_b, w_gate, w_up, w_down):
    shp = h.shape
    x2 = h.reshape(-1, shp[-1])
    n_tok = x2.shape[0]
    lg = (x2 @ rg_w + rg_b).astype(jnp.float32)
    grp = jnp.argmax(lg, axis=-1).astype(jnp.int32)
    w_grp = jnp.take_along_axis(jax.nn.softmax(lg, axis=-1), grp[:, None], axis=-1)
    le = (x2 @ re_w + re_b).astype(jnp.float32).reshape(n_tok, N_GROUPS, EXPERTS_PER_GROUP)
    le = jnp.take_along_axis(le, grp[:, None, None], axis=1)[:, 0]
    top_l, top_j = lax.top_k(le, TOP_K_IN_GROUP)
    gates = jax.nn.softmax(top_l, axis=-1) * w_grp
    experts = grp[:, None] * EXPERTS_PER_GROUP + top_j.astype(jnp.int32)
    y = moe_dispatch(x2, experts, gates.astype(x2.dtype), w_gate, w_up, w_down)
    return y.reshape(shp)


def ple_add(x, p, g_norm, w_gate, w_proj):
    return x + jax.nn.sigmoid(rmsnorm(x, g_norm) @ w_gate) * (p.astype(x.dtype) @ w_proj)


def setup_inputs(seed: int = 0) -> dict:
    key = jax.random.key(seed)
    ks = iter(jax.random.split(key, 48))

    def nrm(shape, scale=1.0):
        return jax.random.normal(next(ks), shape, jnp.float32) * scale

    def gain(shape):
        return 1.0 + nrm(shape, 0.02)

    n_pages = PAST_LEN // PAGE_SIZE
    n_used = DEC_BATCH * n_pages
    n_pool = (5 * n_used + 3) // 4
    perm = jax.random.permutation(next(ks), n_pool)
    page_table = perm[:n_used].reshape(DEC_BATCH, n_pages).astype(jnp.int32)
    d_in = D_MODEL ** -0.5
    return {
        'x_prompt': nrm((BATCH, SEQ, D_MODEL)),
        'x_sample': nrm((DEC_BATCH, DEC_SEQ, D_MODEL)),
        'cache_k': nrm((N_ODD, n_pool, PAGE_SIZE, DIFF_HEADS, 2, DIFF_DH)),
        'cache_v': nrm((N_ODD, n_pool, PAGE_SIZE, DIFF_HEADS, DIFF_DV)),
        'state_gla': nrm((N_EVEN, DEC_BATCH, GLA_HEADS, GLA_DK, GLA_DV)),
        'state_conv': nrm((N_EVEN, DEC_BATCH, CONV_WIDTH - 1, CONV_CH), 0.5),
        'page_table': page_table,
        'p_prompt': nrm((DEPTH, BATCH, SEQ, PLE_DIM)),
        'p_sample': nrm((DEPTH, DEC_BATCH, DEC_SEQ, PLE_DIM)),
        'rel_bias': nrm((REL_BUCKETS, DIFF_HEADS), 0.5),
        'norm_mix': gain((DEPTH, D_MODEL)),
        'norm_ffn': gain((DEPTH, D_MODEL)),
        'norm_ple': gain((DEPTH, D_MODEL)),
        'w_in_ev': nrm((N_EVEN, D_MODEL, EVEN_IN_W), d_in),
        'w_lr_up': nrm((N_EVEN, GLA_RANK, GLA_KEY_W), GLA_RANK ** -0.5),
        'b_lr': nrm((N_EVEN, GLA_KEY_W), 0.1),
        'gla_norm': gain((N_EVEN, GLA_DV)),
        'conv_w': nrm((N_EVEN, CONV_WIDTH, CONV_CH), CONV_WIDTH ** -0.5),
        'conv_b': nrm((N_EVEN, CONV_CH), 0.02),
        'conv_ln_g': gain((N_EVEN, CONV_CH)),
        'conv_ln_b': nrm((N_EVEN, CONV_CH), 0.02),
        'w_out_ev': nrm((N_EVEN, MIX_W, D_MODEL), 0.5 * MIX_W ** -0.5),
        'w_qkv': nrm((N_ODD, D_MODEL, 2 * DIFF_QK_W + DIFF_V_W), d_in),
        'q_norm': gain((N_ODD, DIFF_DH)),
        'k_norm': gain((N_ODD, DIFF_DH)),
        'lam_q1': nrm((N_ODD, DIFF_DH), 0.1),
        'lam_k1': nrm((N_ODD, DIFF_DH), 0.1),
        'lam_q2': nrm((N_ODD, DIFF_DH), 0.1),
        'lam_k2': nrm((N_ODD, DIFF_DH), 0.1),
        'subln': gain((N_ODD, DIFF_DV)),
        'w_out_od': nrm((N_ODD, DIFF_V_W, D_MODEL), 0.5 * DIFF_V_W ** -0.5),
        'router_g_w': nrm((DEPTH, D_MODEL, N_GROUPS), d_in),
        'router_g_b': nrm((DEPTH, N_GROUPS), 0.01),
        'router_e_w': nrm((DEPTH, D_MODEL, N_EXPERTS), d_in),
        'router_e_b': nrm((DEPTH, N_EXPERTS), 0.01),
        'moe_w_gate': nrm((DEPTH, N_EXPERTS, D_MODEL, D_FF_EXPERT), d_in),
        'moe_w_up': nrm((DEPTH, N_EXPERTS, D_MODEL, D_FF_EXPERT), d_in),
        'moe_w_down': nrm((DEPTH, N_EXPERTS, D_FF_EXPERT, D_MODEL), 0.5 * D_FF_EXPERT ** -0.5),
        'ple_proj': nrm((DEPTH, PLE_DIM, D_MODEL), 0.5 * PLE_DIM ** -0.5),
        'ple_gate': nrm((DEPTH, D_MODEL, D_MODEL), d_in),
    }


def reference(x_prompt, x_sample, cache_k, cache_v, state_gla, state_conv, page_table, p_prompt, p_sample,
              rel_bias, norm_mix, norm_ffn, norm_ple, w_in_ev, w_lr_up, b_lr, gla_norm, conv_w, conv_b,
              conv_ln_g, conv_ln_b, w_out_ev, w_qkv, q_norm, k_norm, lam_q1, lam_k1, lam_q2, lam_k2, subln,
              w_out_od, router_g_w, router_g_b, router_e_w, router_e_b, moe_w_gate, moe_w_up, moe_w_down,
              ple_proj, ple_gate):
    xp, xs = x_prompt, x_sample
    bp = xp.shape[0]
    kp_l, vp_l, ks_l, vs_l, gp_l, gs_l, cp_l, cs_l = [], [], [], [], [], [], [], []
    for i in range(DEPTH):
        j = i // 2
        hp = rmsnorm(xp, norm_mix[i])
        hs = rmsnorm(xs, norm_mix[i])
        if i % 2 == 0:
            ew = (w_in_ev[j], w_lr_up[j], b_lr[j], gla_norm[j], conv_w[j], conv_b[j],
                  conv_ln_g[j], conv_ln_b[j], w_out_ev[j])
            gla0 = jnp.zeros((bp,) + state_gla.shape[2:], state_gla.dtype)
            conv0 = jnp.zeros((bp,) + state_conv.shape[2:], state_conv.dtype)
            mp, g_p, c_p = even_mixer(hp, gla0, conv0, *ew)
            ms, g_s, c_s = even_mixer(hs, state_gla[j], state_conv[j], *ew)
            gp_l.append(g_p.astype(state_gla.dtype))
            gs_l.append(g_s.astype(state_gla.dtype))
            cp_l.append(c_p.astype(state_conv.dtype))
            cs_l.append(c_s.astype(state_conv.dtype))
        else:
            lam0 = lambda_init(i)
            lam = diff_lambda(lam_q1[j], lam_k1[j], lam_q2[j], lam_k2[j], lam0)
            qp, kp, vp = diff_project(hp, w_qkv[j], q_norm[j], k_norm[j])
            qs, ks, vs = diff_project(hs, w_qkv[j], q_norm[j], k_norm[j])
            op = diff_attn_prompt(qp, kp, vp, rel_bias, lam)
            osm = diff_attn_sample(qs, ks, vs, cache_k[j, page_table], cache_v[j, page_table], rel_bias, lam)
            mp = diff_finish(op, subln[j], lam0, w_out_od[j])
            ms = diff_finish(osm, subln[j], lam0, w_out_od[j])
            kp_l.append(kp)
            vp_l.append(vp)
            ks_l.append(ks)
            vs_l.append(vs)
        xp = xp + mp
        xs = xs + ms
        mw = (router_g_w[i], router_g_b[i], router_e_w[i], router_e_b[i],
              moe_w_gate[i], moe_w_up[i], moe_w_down[i])
        xp = xp + hier_moe(rmsnorm(xp, norm_ffn[i]), *mw)
        xs = xs + hier_moe(rmsnorm(xs, norm_ffn[i]), *mw)
        xp = ple_add(xp, p_prompt[i], norm_ple[i], ple_gate[i], ple_proj[i])
        xs = ple_add(xs, p_sample[i], norm_ple[i], ple_gate[i], ple_proj[i])
    k_prompt = jnp.stack(kp_l)
    v_prompt = jnp.stack(vp_l)
    k_sample = jnp.stack(ks_l)
    v_sample = jnp.stack(vs_l)
    gla_prompt = jnp.stack(gp_l)
    gla_sample = jnp.stack(gs_l)
    conv_prompt = jnp.stack(cp_l)
    conv_sample = jnp.stack(cs_l)
    return (xp, xs, k_prompt, v_prompt, k_sample, v_sample, gla_prompt, gla_sample, conv_prompt, conv_sample)
```

```python
import functools
import math

import numpy as np
import jax
import jax.numpy as jnp
from jax import lax
from jax.experimental import pallas as pl
from jax.experimental.pallas import tpu as pltpu

F32 = jnp.float32
BF16 = jnp.bfloat16

D_MODEL = 1024
NORM_EPS = 1e-6
GLA_HEADS = 4
GLA_DK = 64
GLA_DV = 128
GLA_KEY_W = GLA_HEADS * GLA_DK
GLA_VAL_W = GLA_HEADS * GLA_DV
GLA_RANK = 16
GLA_TAU = 16.0
GLA_CHUNK = 64
CONV_CH = 512
CONV_WIDTH = 31
CONV_HALO = 32
DIFF_HEADS = 8
DIFF_DH = 64
DIFF_DV = 128
REL_BUCKETS = 32
REL_MAX_DIST = 128
ATT_TILE = 256
MASK_VALUE = -1e30
N_GROUPS = 4
EXPERTS_PER_GROUP = 8
N_EXPERTS = N_GROUPS * EXPERTS_PER_GROUP
D_FF = 512
MOE_TILE = 256
ROUTE_LANES = 128
PLE_DIM = 256

LANES = 128
VMEM_LIMIT_BYTES = 56 * 1024 * 1024
EVEN_Z_W = 2 * GLA_KEY_W + 2 * GLA_VAL_W + 2 * CONV_CH + LANES
EVEN_LR_BLOCK = (2 * GLA_KEY_W + 2 * GLA_VAL_W + 2 * CONV_CH) // LANES


def _params(*sem):
    return pltpu.CompilerParams(dimension_semantics=sem, vmem_limit_bytes=VMEM_LIMIT_BYTES)


def _dot(a, b):
    return jnp.dot(a, b, preferred_element_type=F32)


def _dot_nt(a, b):
    return lax.dot_general(a, b, (((1,), (1,)), ((), ())), preferred_element_type=F32)


def _dot_tn(a, b):
    return lax.dot_general(a, b, (((0,), (0,)), ((), ())), preferred_element_type=F32)


def _split_bf16(x):
    hi = x.astype(BF16)
    lo = (x - hi.astype(F32)).astype(BF16)
    return hi, lo


def _rmsnorm(x, g):
    ms = jnp.mean(x * x, axis=-1, keepdims=True)
    return x * lax.rsqrt(ms + NORM_EPS) * g


def _sigmoid(x):
    return 1.0 / (1.0 + jnp.exp(-x))


def _silu(x):
    return x * _sigmoid(x)


def _row_tile(n_rows, want):
    t = min(want, n_rows)
    assert n_rows % t == 0, (n_rows, t)
    return t


def _norm_proj_kernel(x_ref, g_ref, w_ref, o_ref):
    h = _rmsnorm(x_ref[...], g_ref[...]).astype(BF16)
    o_ref[...] = _dot(h, w_ref[...])


def norm_proj(x, g, w_bf16, tm=256):
    t, d = x.shape
    n = w_bf16.shape[1]
    tm = _row_tile(t, tm)
    return pl.pallas_call(
        _norm_proj_kernel,
        out_shape=jax.ShapeDtypeStruct((t, n), F32),
        grid=(t // tm,),
        in_specs=[pl.BlockSpec((tm, d), lambda i: (i, 0)),
                  pl.BlockSpec((1, d), lambda i: (0, 0)),
                  pl.BlockSpec((d, n), lambda i: (0, 0))],
        out_specs=pl.BlockSpec((tm, n), lambda i: (i, 0)),
        compiler_params=_params("parallel"),
        name="norm_proj",
    )(x, g.reshape(1, d), w_bf16)


def _proj_residual_kernel(*refs, n_in):
    a_refs = refs[:n_in]
    w_ref, x_ref, o_ref = refs[n_in:]
    acc = x_ref[...]
    row = 0
    for a_ref in a_refs:
        k = a_ref.shape[1]
        acc = acc + _dot(a_ref[...].astype(BF16), w_ref[row:row + k, :])
        row += k
    o_ref[...] = acc


def proj_residual(acts, w_bf16, x, tm=512):
    t, d = x.shape
    tm = _row_tile(t, tm)
    n_in = len(acts)
    in_specs = [pl.BlockSpec((tm, a.shape[1]), lambda i: (i, 0)) for a in acts]
    in_specs += [pl.BlockSpec(w_bf16.shape, lambda i: (0, 0)),
                 pl.BlockSpec((tm, d), lambda i: (i, 0))]
    return pl.pallas_call(
        functools.partial(_proj_residual_kernel, n_in=n_in),
        out_shape=jax.ShapeDtypeStruct((t, d), F32),
        grid=(t // tm,),
        in_specs=in_specs,
        out_specs=pl.BlockSpec((tm, d), lambda i: (i, 0)),
        compiler_params=_params("parallel"),
        name="proj_residual",
    )(*acts, w_bf16, x)


def _ple_kernel(x_ref, p_ref, g_ref, wg_ref, wp_ref, o_ref):
    x = x_ref[...]
    h = _rmsnorm(x, g_ref[...]).astype(BF16)
    gate = _sigmoid(_dot(h, wg_ref[...]))
    proj = _dot(p_ref[...].astype(BF16), wp_ref[...])
    o_ref[...] = x + gate * proj


def ple_add(x, p, g, wg_bf16, wp_bf16, tm=512):
    t, d = x.shape
    tm = _row_tile(t, tm)
    pd = p.shape[1]
    return pl.pallas_call(
        _ple_kernel,
        out_shape=jax.ShapeDtypeStruct((t, d), F32),
        grid=(t // tm,),
        in_specs=[pl.BlockSpec((tm, d), lambda i: (i, 0)),
                  pl.BlockSpec((tm, pd), lambda i: (i, 0)),
                  pl.BlockSpec((1, d), lambda i: (0, 0)),
                  pl.BlockSpec((d, d), lambda i: (0, 0)),
                  pl.BlockSpec((pd, d), lambda i: (0, 0))],
        out_specs=pl.BlockSpec((tm, d), lambda i: (i, 0)),
        compiler_params=_params("parallel"),
        name="ple_add",
    )(x, p, g.reshape(1, d), wg_bf16, wp_bf16)


def _gla_constants(chunk):
    c = chunk
    t = np.arange(c)
    tri = (t[None, :] <= t[:, None]).astype(np.float32)
    causal4 = np.tile(tri, (GLA_HEADS, 1))
    lane_head = np.arange(GLA_KEY_W) // GLA_DK
    row_head = np.repeat(np.arange(GLA_HEADS), c)
    headmask4 = (row_head[:, None] == lane_head[None, :]).astype(np.float32)
    blockdiag = (np.arange(GLA_KEY_W)[:, None] // GLA_DK
                 == np.arange(GLA_VAL_W)[None, :] // GLA_DV).astype(np.float32)
    return (jnp.asarray(tri, BF16), jnp.asarray(causal4), jnp.asarray(headmask4), jnp.asarray(blockdiag))


def _gla_kernel(*refs, chunk, n_chunks, has_state, valid_rows):
    if has_state:
        (qk_ref, v_ref, r_ref, lr_ref, wlr_ref, blr_ref, gg_ref, tri_ref, causal_ref, hmask_ref, bd_ref,
         s0_ref, o_ref, sout_ref, s_scr) = refs
    else:
        (qk_ref, v_ref, r_ref, lr_ref, wlr_ref, blr_ref, gg_ref, tri_ref, causal_ref, hmask_ref, bd_ref,
         o_ref, sout_ref, s_scr) = refs
    c = chunk
    i = pl.program_id(1)
    bd = bd_ref[...]

    @pl.when(i == 0)
    def _init():
        if has_state:
            s0 = s0_ref[0]
            s_scr[...] = jnp.concatenate([s0] * GLA_HEADS, axis=1) * bd
        else:
            s_scr[...] = jnp.zeros_like(s_scr)

    tri = tri_ref[...]
    ones = jnp.ones((c, LANES), BF16)

    def chunk_step(ci, carry):
        rows = pl.ds(pl.multiple_of(ci * c, c), c)
        qk = qk_ref[rows, :]
        q = qk[:, :GLA_KEY_W] * (GLA_DK ** -0.5)
        k = qk[:, GLA_KEY_W:]
        vb = v_ref[rows, :].astype(BF16)
        x = _dot(lr_ref[rows, :].astype(BF16), wlr_ref[...]) + blr_ref[...]
        g = (jnp.minimum(x, 0.0) - jnp.log1p(jnp.exp(-jnp.abs(x)))) * (1.0 / GLA_TAU)
        if valid_rows < c:
            g = jnp.where(lax.broadcasted_iota(jnp.int32, g.shape, 0) < valid_rows, g, 0.0)
        g_hi, g_lo = _split_bf16(g)
        b = _dot(tri, g_hi) + _dot(tri, g_lo)
        dcol = _dot_tn(g_hi, ones) + _dot_tn(g_lo, ones)
        bend = b[c - 1:c, :]
        half = 0.5 * bend
        qs = q * jnp.exp(b - half)
        ks = k * jnp.exp(half - b)
        qt = q * jnp.exp(b)
        kh = k * jnp.exp(bend - b)
        qstack = (jnp.concatenate([qs] * GLA_HEADS, axis=0) * hmask_ref[...]).astype(BF16)
        scores = (_dot_nt(qstack, ks.astype(BF16)) * causal_ref[...]).astype(BF16)
        s_bd = s_scr[...]
        o_inter = _dot(qt.astype(BF16), s_bd.astype(BF16))
        gg = gg_ref[...]
        for h in range(GLA_HEADS):
            cols = slice(h * GLA_DV, (h + 1) * GLA_DV)
            o_h = _dot(scores[h * c:(h + 1) * c, :], vb[:, cols]) + o_inter[:, cols]
            y = _rmsnorm(o_h, gg) * _silu(r_ref[rows, cols])
            o_ref[rows, cols] = y.astype(o_ref.dtype)
        decay = jnp.concatenate([jnp.exp(dcol)] * GLA_HEADS, axis=1)
        s_scr[...] = s_bd * decay + _dot_tn(kh.astype(BF16), vb) * bd
        return carry

    lax.fori_loop(0, n_chunks, chunk_step, 0)

    @pl.when(i == pl.num_programs(1) - 1)
    def _final():
        s_bd = s_scr[...]
        for h in range(GLA_HEADS):
            sout_ref[0, h * GLA_DK:(h + 1) * GLA_DK, :] = s_bd[h * GLA_DK:(h + 1) * GLA_DK,
                                                               h * GLA_DV:(h + 1) * GLA_DV]


def gla_mix(z, n_batch, seq, chunk, tile, w_lr_pad, b_lr, gla_g, state0, valid_rows=None):
    nt = seq // tile
    assert seq % tile == 0 and tile % chunk == 0
    valid_rows = chunk if valid_rows is None else valid_rows
    assert valid_rows == chunk or seq == chunk
    has_state = state0 is not None
    consts = _gla_constants(chunk)
    row = lambda b, i: b * nt + i
    in_specs = [pl.BlockSpec((tile, 2 * GLA_KEY_W), lambda b, i: (row(b, i), 0)),
                pl.BlockSpec((tile, GLA_VAL_W), lambda b, i: (row(b, i), 1)),
                pl.BlockSpec((tile, GLA_VAL_W), lambda b, i: (row(b, i), 2)),
                pl.BlockSpec((tile, LANES), lambda b, i: (row(b, i), EVEN_LR_BLOCK)),
                pl.BlockSpec(w_lr_pad.shape, lambda b, i: (0, 0)),
                pl.BlockSpec((1, GLA_KEY_W), lambda b, i: (0, 0)),
                pl.BlockSpec((1, GLA_DV), lambda b, i: (0, 0))]
    in_specs += [pl.BlockSpec(a.shape, lambda b, i: (0, 0)) for a in consts]
    args = [z, z, z, z, w_lr_pad, b_lr.reshape(1, GLA_KEY_W), gla_g.reshape(1, GLA_DV), *consts]
    if has_state:
        in_specs.append(pl.BlockSpec((1, GLA_KEY_W, GLA_DV), lambda b, i: (b, 0, 0)))
        args.append(state0)
    return pl.pallas_call(
        functools.partial(_gla_kernel, chunk=chunk, n_chunks=tile // chunk, has_state=has_state,
                          valid_rows=valid_rows),
        out_shape=(jax.ShapeDtypeStruct((n_batch * seq, GLA_VAL_W), BF16),
                   jax.ShapeDtypeStruct((n_batch, GLA_KEY_W, GLA_DV), F32)),
        grid=(n_batch, nt),
        in_specs=in_specs,
        out_specs=(pl.BlockSpec((tile, GLA_VAL_W), lambda b, i: (row(b, i), 0)),
                   pl.BlockSpec((1, GLA_KEY_W, GLA_DV), lambda b, i: (b, 0, 0))),
        scratch_shapes=[pltpu.VMEM((GLA_KEY_W, GLA_VAL_W), F32)],
        compiler_params=_params("parallel", "arbitrary"),
        name="gla_mix",
    )(*args)


def _layernorm_silu(c, g, b):
    mu = jnp.mean(c, axis=-1, keepdims=True)
    cc = c - mu
    var = jnp.mean(cc * cc, axis=-1, keepdims=True)
    return _silu(cc * lax.rsqrt(var + NORM_EPS) * g + b)


def _conv_kernel(a_ref, gt_ref, cw_ref, cb_ref, lg_ref, lb_ref, o_ref, st_ref, ext, *, tile, sub):
    i = pl.program_id(1)
    n_state = CONV_WIDTH - 1

    @pl.when(i == 0)
    def _zero_halo():
        ext[0:CONV_HALO, :] = jnp.zeros((CONV_HALO, CONV_CH), F32)

    @pl.when(i > 0)
    def _carry_halo():
        ext[0:CONV_HALO, :] = ext[tile:tile + CONV_HALO, :]

    ext[CONV_HALO:CONV_HALO + tile, :] = a_ref[...] * _sigmoid(gt_ref[...])
    lg = lg_ref[...]
    lb = lb_ref[...]
    for s in range(tile // sub):
        acc = jnp.broadcast_to(cb_ref[...], (sub, CONV_CH))
        for w in range(CONV_WIDTH):
            off = s * sub + (CONV_HALO - n_state) + w
            acc = acc + ext[off:off + sub, :] * cw_ref[w:w + 1, :]
        o_ref[s * sub:(s + 1) * sub, :] = _layernorm_silu(acc, lg, lb).astype(o_ref.dtype)

    @pl.when(i == pl.num_programs(1) - 1)
    def _state():
        st_ref[0] = ext[tile + CONV_HALO - n_state:tile + CONV_HALO, :]


def conv_mix(z, n_batch, seq, conv_w_pad, conv_b, ln_g, ln_b, tile=256, sub=32):
    nt = seq // tile
    assert seq % tile == 0 and tile % sub == 0 and tile >= CONV_HALO
    row = lambda b, i: b * nt + i
    vec = lambda a: a.reshape(1, CONV_CH)
    return pl.pallas_call(
        functools.partial(_conv_kernel, tile=tile, sub=sub),
        out_shape=(jax.ShapeDtypeStruct((n_batch * seq, CONV_CH), BF16),
                   jax.ShapeDtypeStruct((n_batch, CONV_WIDTH - 1, CONV_CH), F32)),
        grid=(n_batch, nt),
        in_specs=[pl.BlockSpec((tile, CONV_CH), lambda b, i: (row(b, i), 3)),
                  pl.BlockSpec((tile, CONV_CH), lambda b, i: (row(b, i), 4)),
                  pl.BlockSpec(conv_w_pad.shape, lambda b, i: (0, 0)),
                  pl.BlockSpec((1, CONV_CH), lambda b, i: (0, 0)),
                  pl.BlockSpec((1, CONV_CH), lambda b, i: (0, 0)),
                  pl.BlockSpec((1, CONV_CH), lambda b, i: (0, 0))],
        out_specs=(pl.BlockSpec((tile, CONV_CH), lambda b, i: (row(b, i), 0)),
                   pl.BlockSpec((1, CONV_WIDTH - 1, CONV_CH), lambda b, i: (b, 0, 0))),
        scratch_shapes=[pltpu.VMEM((tile + CONV_HALO, CONV_CH), F32)],
        compiler_params=_params("parallel", "arbitrary"),
        name="conv_mix",
    )(z, z, conv_w_pad, vec(conv_b), vec(ln_g), vec(ln_b))


def _conv_step_kernel(a_ref, gt_ref, st_ref, cw_ref, cb_ref, lg_ref, lb_ref, o_ref, sto_ref):
    n_state = CONV_WIDTH - 1
    u = a_ref[...] * _sigmoid(gt_ref[...])
    acc = cb_ref[...] + u * cw_ref[n_state:n_state + 1, :]
    for w in range(n_state):
        acc = acc + st_ref[:, w * CONV_CH:(w + 1) * CONV_CH] * cw_ref[w:w + 1, :]
    o_ref[...] = _layernorm_silu(acc, lg_ref[...], lb_ref[...]).astype(o_ref.dtype)
    sto_ref[:, 0:(n_state - 1) * CONV_CH] = st_ref[:, CONV_CH:n_state * CONV_CH]
    sto_ref[:, (n_state - 1) * CONV_CH:] = u


def conv_step(z, state, conv_w_pad, conv_b, ln_g, ln_b, tile=32):
    nb = z.shape[0]
    n_state = CONV_WIDTH - 1
    tile = _row_tile(nb, tile)
    vec = lambda a: a.reshape(1, CONV_CH)
    st2 = state.reshape(nb, n_state * CONV_CH)
    out, st_new = pl.pallas_call(
        _conv_step_kernel,
        out_shape=(jax.ShapeDtypeStruct((nb, CONV_CH), BF16),
                   jax.ShapeDtypeStruct((nb, n_state * CONV_CH), F32)),
        grid=(nb // tile,),
        in_specs=[pl.BlockSpec((tile, CONV_CH), lambda i: (i, 3)),
                  pl.BlockSpec((tile, CONV_CH), lambda i: (i, 4)),
                  pl.BlockSpec((tile, n_state * CONV_CH), lambda i: (i, 0)),
                  pl.BlockSpec(conv_w_pad.shape, lambda i: (0, 0)),
                  pl.BlockSpec((1, CONV_CH), lambda i: (0, 0)),
                  pl.BlockSpec((1, CONV_CH), lambda i: (0, 0)),
                  pl.BlockSpec((1, CONV_CH), lambda i: (0, 0))],
        out_specs=(pl.BlockSpec((tile, CONV_CH), lambda i: (i, 0)),
                   pl.BlockSpec((tile, n_state * CONV_CH), lambda i: (i, 0))),
        compiler_params=_params("parallel"),
        name="conv_step",
    )(z, z, st2, conv_w_pad, vec(conv_b), vec(ln_g), vec(ln_b))
    return out, st_new.reshape(nb, n_state, CONV_CH)


def _group_constants():
    grp = np.zeros((D_MODEL, LANES), np.float32)
    grp[np.arange(D_MODEL), np.arange(D_MODEL) // DIFF_DH] = 1.0
    return jnp.asarray(grp, BF16), jnp.asarray(grp.T.copy(), BF16)


def _qkv_kernel(x_ref, g_ref, w_ref, grp_ref, grpt_ref, qg_ref, kg_ref, q_ref, k_ref, v_ref, kb_ref, vb_ref):
    h = _rmsnorm(x_ref[...], g_ref[...]).astype(BF16)
    z = _dot(h, w_ref[...])
    grp = grp_ref[...]
    grpt = grpt_ref[...]

    def map_norm(y, gain):
        hi, lo = _split_bf16(y * y)
        ms = (_dot(hi, grp) + _dot(lo, grp)) * (1.0 / DIFF_DH)
        rh, rl = _split_bf16(lax.rsqrt(ms + NORM_EPS))
        return y * (_dot(rh, grpt) + _dot(rl, grpt)) * gain

    qn = map_norm(z[:, :D_MODEL], qg_ref[...])
    kn = map_norm(z[:, D_MODEL:2 * D_MODEL], kg_ref[...])
    v = z[:, 2 * D_MODEL:]
    q_ref[...] = (qn * (DIFF_DH ** -0.5)).astype(BF16)
    k_ref[...] = kn
    v_ref[...] = v
    kb_ref[...] = kn.astype(BF16)
    vb_ref[...] = v.astype(BF16)


def qkv_proj(x, g, w_bf16, q_gain, k_gain, tm=256):
    t, d = x.shape
    tm = _row_tile(t, tm)
    grp, grpt = _group_constants()
    tile_gain = lambda a: jnp.tile(a, D_MODEL // DIFF_DH).reshape(1, D_MODEL)
    row = pl.BlockSpec((tm, d), lambda i: (i, 0))
    full = lambda a: pl.BlockSpec(a.shape, lambda i: (0, 0))
    qg, kg = tile_gain(q_gain), tile_gain(k_gain)
    return pl.pallas_call(
        _qkv_kernel,
        out_shape=(jax.ShapeDtypeStruct((t, d), BF16), jax.ShapeDtypeStruct((t, d), F32),
                   jax.ShapeDtypeStruct((t, d), F32), jax.ShapeDtypeStruct((t, d), BF16),
                   jax.ShapeDtypeStruct((t, d), BF16)),
        grid=(t // tm,),
        in_specs=[row, pl.BlockSpec((1, d), lambda i: (0, 0)), full(w_bf16), full(grp), full(grpt),
                  full(qg), full(kg)],
        out_specs=(row, row, row, row, row),
        compiler_params=_params("parallel"),
        name="qkv_proj",
    )(x, g.reshape(1, d), w_bf16, grp, grpt, qg, kg)


def _rel_bucket_np(n):
    max_exact = REL_BUCKETS // 2
    nf = np.maximum(n, 1).astype(np.float64)
    large = max_exact + (np.log(nf / max_exact) / math.log(REL_MAX_DIST / max_exact)
                         * (REL_BUCKETS - max_exact)).astype(np.int32)
    return np.where(n < max_exact, np.maximum(n, 0), np.minimum(large, REL_BUCKETS - 1))


def _diff_lambda(lamv, lam0):
    s1 = jnp.sum(lamv[0:1, :] * lamv[1:2, :], axis=-1, keepdims=True)
    s2 = jnp.sum(lamv[2:3, :] * lamv[3:4, :], axis=-1, keepdims=True)
    return jnp.exp(s1) - jnp.exp(s2) + lam0


def _attn_kernel(q_ref, k_ref, v_ref, bdiag_ref, bsub_ref, bfar_ref, lamv_ref, sub_ref, o_ref,
                 m_scr, l_scr, acc_scr, *, tile, lam0):
    qi = pl.program_id(2)
    q = q_ref[...]
    lane = lax.broadcasted_iota(jnp.int32, q.shape, 1)
    zero = jnp.zeros_like(q)
    q_maps = (jnp.where(lane < DIFF_DH, q, zero), jnp.where(lane >= DIFF_DH, q, zero))
    reps = tile // LANES

    m_scr[...] = jnp.full(m_scr.shape, -jnp.inf, F32)
    l_scr[...] = jnp.zeros(l_scr.shape, F32)
    acc_scr[...] = jnp.zeros(acc_scr.shape, F32)

    def flash_step(kt, bias):
        rows = pl.ds(pl.multiple_of(kt * tile, tile), tile)
        k = k_ref[rows, :]
        v = v_ref[rows, :]
        for m in range(2):
            s = _dot_nt(q_maps[m], k) + bias
            m_prev = m_scr[m]
            m_new = jnp.maximum(m_prev, jnp.max(s, axis=1, keepdims=True))
            alpha = jnp.exp(m_prev - m_new)
            p = jnp.exp(s - jnp.concatenate([m_new] * reps, axis=1))
            l_scr[m] = alpha * l_scr[m] + jnp.sum(p, axis=1, keepdims=True)
            acc_scr[m] = alpha * acc_scr[m] + _dot(p.astype(BF16), v)
            m_scr[m] = m_new

    def far_step(kt, carry):
        flash_step(kt, bfar_ref[0])
        return carry

    lax.fori_loop(0, jnp.maximum(qi - 1, 0), far_step, 0)

    @pl.when(qi >= 1)
    def _sub():
        flash_step(qi - 1, bsub_ref[0])

    flash_step(qi, bdiag_ref[0])

    lam = _diff_lambda(lamv_ref[...], lam0)
    o = acc_scr[0] / l_scr[0] - lam * (acc_scr[1] / l_scr[1])
    o_ref[...] = (_rmsnorm(o, sub_ref[...]) * (1.0 - lam0)).astype(o_ref.dtype)


def _prompt_bias_tables(rel_bias, tile):
    r = np.arange(tile)
    n_diag = r[:, None] - r[None, :]
    n_sub = n_diag + tile
    hb = rel_bias.T
    bdiag = jnp.where(jnp.asarray(n_diag >= 0)[None], hb[:, _rel_bucket_np(n_diag)], MASK_VALUE)
    bsub = hb[:, _rel_bucket_np(n_sub)]
    assert tile + 1 >= REL_MAX_DIST
    bfar = jnp.broadcast_to(hb[:, REL_BUCKETS - 1][:, None, None], (DIFF_HEADS, 1, tile))
    return bdiag.astype(F32), bsub.astype(F32), bfar.astype(F32)


def diff_attention(q, kb, vb, n_batch, seq, rel_bias, lamv, subln, lam0, tile=ATT_TILE):
    nq = seq // tile
    assert seq % tile == 0
    bdiag, bsub, bfar = _prompt_bias_tables(rel_bias, tile)
    head_tile = lambda b, h, i: (h, 0, 0)
    return pl.pallas_call(
        functools.partial(_attn_kernel, tile=tile, lam0=lam0),
        out_shape=jax.ShapeDtypeStruct((n_batch * seq, D_MODEL), BF16),
        grid=(n_batch, DIFF_HEADS, nq),
        in_specs=[pl.BlockSpec((tile, DIFF_DV), lambda b, h, i: (b * nq + i, h)),
                  pl.BlockSpec((seq, DIFF_DV), lambda b, h, i: (b, h)),
                  pl.BlockSpec((seq, DIFF_DV), lambda b, h, i: (b, h)),
                  pl.BlockSpec((1, tile, tile), head_tile),
                  pl.BlockSpec((1, tile, tile), head_tile),
                  pl.BlockSpec((1, 1, tile), head_tile),
                  pl.BlockSpec((4, DIFF_DH), lambda b, h, i: (0, 0)),
                  pl.BlockSpec((1, DIFF_DV), lambda b, h, i: (0, 0))],
        out_specs=pl.BlockSpec((tile, DIFF_DV), lambda b, h, i: (b * nq + i, h)),
        scratch_shapes=[pltpu.VMEM((2, tile, LANES), F32), pltpu.VMEM((2, tile, LANES), F32),
                        pltpu.VMEM((2, tile, DIFF_DV), F32)],
        compiler_params=_params("parallel", "parallel", "arbitrary"),
        name="diff_attention",
    )(q, kb, vb, bdiag, bsub, bfar, lamv, subln.reshape(1, DIFF_DV))


def _step_constants(n_past, page, rel_bias):
    n_maps = 2 * DIFF_HEADS
    j = np.arange(LANES)
    c = np.arange(D_MODEL)
    col_head, col_map = c // DIFF_DV, (c % DIFF_DV) // DIFF_DH
    sel = ((j[:, None] < n_maps) & (j[:, None] % DIFF_HEADS == col_head[None, :])
           & (j[:, None] // DIFF_HEADS == col_map[None, :])).astype(np.float32)
    expand = ((j[:, None] < n_maps) & (j[:, None] % DIFF_HEADS == col_head[None, :])).astype(np.float32)
    lane_head = j % DIFF_HEADS
    n = n_past - np.arange(n_past)
    past_bias = rel_bias[_rel_bucket_np(n)][:, lane_head]
    new_bias = rel_bias[_rel_bucket_np(np.zeros((1,), np.int64))][:, lane_head]
    return jnp.asarray(sel), jnp.asarray(expand, BF16), past_bias.astype(F32), new_bias.astype(F32)


def _attn_step_kernel(pt_ref, q_ref, kn_ref, vn_ref, *refs, n_pages, page, lam0):
    k_refs = refs[:n_pages]
    v_refs = refs[n_pages:2 * n_pages]
    (bias_ref, nbias_ref, sel_ref, exp_ref, lamv_ref, sub_ref, o_ref, s_scr) = refs[2 * n_pages:]
    n_maps = 2 * DIFF_HEADS
    pad_rows = 16
    q_rows = sel_ref[...] * q_ref[0].astype(F32)
    q_mat = q_rows.T.astype(BF16)
    for p in range(n_pages):
        s_scr[p * page:(p + 1) * page, :] = (_dot(k_refs[p][...].astype(BF16), q_mat)
                                             + bias_ref[p * page:(p + 1) * page, :])
    k_new = jnp.broadcast_to(kn_ref[0], (pad_rows, D_MODEL)).astype(BF16)
    s_new = _dot(k_new, q_mat)[0:1, :] + nbias_ref[...]
    s = s_scr[...]
    m = jnp.maximum(jnp.max(s, axis=0, keepdims=True), s_new)
    p_all = jnp.exp(s - m)
    p_new = jnp.exp(s_new - m)
    s_scr[...] = p_all
    l = jnp.sum(p_all, axis=0, keepdims=True) + p_new
    lam = _diff_lambda(lamv_ref[...], lam0)
    lane = lax.broadcasted_iota(jnp.int32, (1, LANES), 1)
    coef = jnp.where(lane < DIFF_HEADS, 1.0 / l, jnp.where(lane < n_maps, -lam / l, 0.0))
    expand = exp_ref[...]
    acc = jnp.zeros((8, D_MODEL), F32)
    for p in range(n_pages):
        w = _dot((s_scr[p * page:(p + 1) * page, :] * coef).astype(BF16), expand)
        acc = acc + jnp.sum((w * v_refs[p][...]).reshape(page // 8, 8, D_MODEL), axis=0)
    w_new = _dot(jnp.broadcast_to(p_new * coef, (pad_rows, LANES)).astype(BF16), expand)[0:1, :]
    o = jnp.sum(acc, axis=0, keepdims=True) + w_new * vn_ref[0]
    sub = sub_ref[...]
    for h in range(DIFF_HEADS):
        cols = slice(h * DIFF_DV, (h + 1) * DIFF_DV)
        o_ref[0, :, cols] = (_rmsnorm(o[:, cols], sub) * (1.0 - lam0)).astype(o_ref.dtype)


def diff_attention_step(q, k_new, v_new, cache_k, cache_v, layer, page_table, rel_bias, lamv, subln, lam0):
    nb = q.shape[0]
    n_pages = page_table.shape[1]
    page = cache_k.shape[2]
    n_past = n_pages * page
    sel, expand, past_bias, new_bias = _step_constants(n_past, page, rel_bias)
    tok = lambda a: a.reshape(nb, 1, D_MODEL)
    tok_spec = pl.BlockSpec((1, 1, D_MODEL), lambda b, pt: (b, 0, 0))
    page_spec = lambda p: pl.BlockSpec((None, None, page, D_MODEL),
                                       lambda b, pt: (layer, pt[b * n_pages + p], 0, 0))
    full = lambda a: pl.BlockSpec(a.shape, lambda b, pt: (0,) * a.ndim)
    sub2 = subln.reshape(1, DIFF_DV)
    grid_spec = pltpu.PrefetchScalarGridSpec(
        num_scalar_prefetch=1,
        grid=(nb,),
        in_specs=([tok_spec, tok_spec, tok_spec]
                  + [page_spec(p) for p in range(n_pages)] + [page_spec(p) for p in range(n_pages)]
                  + [full(past_bias), full(new_bias), full(sel), full(expand), full(lamv), full(sub2)]),
        out_specs=tok_spec,
        scratch_shapes=[pltpu.VMEM((n_past, LANES), F32)],
    )
    out = pl.pallas_call(
        functools.partial(_attn_step_kernel, n_pages=n_pages, page=page, lam0=lam0),
        out_shape=jax.ShapeDtypeStruct((nb, 1, D_MODEL), F32),
        grid_spec=grid_spec,
        compiler_params=_params("parallel"),
        name="diff_attention_step",
    )(page_table.reshape(-1), tok(q), tok(k_new), tok(v_new),
      *([cache_k] * n_pages), *([cache_v] * n_pages), past_bias, new_bias, sel, expand, lamv, sub2)
    return out.reshape(nb, D_MODEL)


def _router_kernel(x_ref, g_ref, w_ref, b_ref, h_ref, r_ref):
    h = _rmsnorm(x_ref[...], g_ref[...])
    h_ref[...] = h.astype(BF16)
    logits = jnp.dot(h, w_ref[...], precision=lax.Precision.HIGHEST, preferred_element_type=F32) + b_ref[...]
    lane = lax.broadcasted_iota(jnp.int32, logits.shape, 1)
    neg = jnp.float32(-jnp.inf)
    big = jnp.int32(2 ** 30)

    def first_argmax(vals):
        mx = jnp.max(vals, axis=1, keepdims=True)
        idx = jnp.min(jnp.where(vals == mx, lane, big), axis=1, keepdims=True)
        return mx, idx

    lg = jnp.where(lane < N_GROUPS, logits, neg)
    g_max, grp = first_argmax(lg)
    w_grp = 1.0 / jnp.sum(jnp.exp(lg - g_max), axis=1, keepdims=True)
    e_lane = lane - N_GROUPS
    in_grp = (e_lane >= grp * EXPERTS_PER_GROUP) & (e_lane < (grp + 1) * EXPERTS_PER_GROUP)
    le = jnp.where(in_grp, logits, neg)
    m1, i1 = first_argmax(le)
    m2, i2 = first_argmax(jnp.where(lane == i1, neg, le))
    e2 = jnp.exp(m2 - m1)
    gate1 = w_grp / (1.0 + e2)
    gate2 = w_grp * e2 / (1.0 + e2)
    f = lambda a: a.astype(F32)
    r_ref[...] = jnp.where(lane == 0, f(i1 - N_GROUPS),
                           jnp.where(lane == 1, f(i2 - N_GROUPS),
                                     jnp.where(lane == 2, gate1, jnp.where(lane == 3, gate2, 0.0))))


def moe_router(x, g, w_route, b_route, tm=512):
    t, d = x.shape
    tm = _row_tile(t, tm)
    return pl.pallas_call(
        _router_kernel,
        out_shape=(jax.ShapeDtypeStruct((t, d), BF16), jax.ShapeDtypeStruct((t, ROUTE_LANES), F32)),
        grid=(t // tm,),
        in_specs=[pl.BlockSpec((tm, d), lambda i: (i, 0)),
                  pl.BlockSpec((1, d), lambda i: (0, 0)),
                  pl.BlockSpec((d, ROUTE_LANES), lambda i: (0, 0)),
                  pl.BlockSpec((1, ROUTE_LANES), lambda i: (0, 0))],
        out_specs=(pl.BlockSpec((tm, d), lambda i: (i, 0)),
                   pl.BlockSpec((tm, ROUTE_LANES), lambda i: (i, 0))),
        compiler_params=_params("parallel"),
        name="moe_router",
    )(x, g.reshape(1, d), w_route, b_route)


def _expert_kernel(be_ref, nu_ref, x_ref, wg_ref, wu_ref, wd_ref, o_ref, wg_s, wu_s, wd_s):
    b = pl.program_id(0)
    prev = be_ref[jnp.maximum(b - 1, 0)]

    @pl.when((b == 0) | (be_ref[b] != prev))
    def _load_expert():
        wg_s[...] = wg_ref[...].astype(BF16)
        wu_s[...] = wu_ref[...].astype(BF16)
        wd_s[...] = wd_ref[...].astype(BF16)

    @pl.when(b < nu_ref[0])
    def _compute():
        x = x_ref[...]
        mid = _silu(_dot(x, wg_s[...])) * _dot(x, wu_s[...])
        o_ref[...] = _dot(mid.astype(BF16), wd_s[...])

    @pl.when(b >= nu_ref[0])
    def _unused():
        o_ref[...] = jnp.zeros_like(o_ref)


def moe_experts(xb, blk_e, n_used, w_gate, w_up, w_down, layer, tb=MOE_TILE):
    n_rows, d = xb.shape
    n_blk = n_rows // tb
    w_in_spec = pl.BlockSpec((None, None, d, D_FF), lambda b, be, nu: (layer, be[b], 0, 0))
    grid_spec = pltpu.PrefetchScalarGridSpec(
        num_scalar_prefetch=2,
        grid=(n_blk,),
        in_specs=[pl.BlockSpec((tb, d), lambda b, be, nu: (b, 0)),
                  w_in_spec, w_in_spec,
                  pl.BlockSpec((None, None, D_FF, d), lambda b, be, nu: (layer, be[b], 0, 0))],
        out_specs=pl.BlockSpec((tb, d), lambda b, be, nu: (b, 0)),
        scratch_shapes=[pltpu.VMEM((d, D_FF), BF16), pltpu.VMEM((d, D_FF), BF16), pltpu.VMEM((D_FF, d), BF16)],
    )
    return pl.pallas_call(
        _expert_kernel,
        out_shape=jax.ShapeDtypeStruct((n_rows, d), F32),
        grid_spec=grid_spec,
        compiler_params=_params("arbitrary"),
        name="moe_experts",
    )(blk_e, n_used, xb, w_gate, w_up, w_down)


def hier_moe_add(x, layer, norm_g, rg_w, rg_b, re_w, re_b, w_gate, w_up, w_down, tb=MOE_TILE):
    t, d = x.shape
    w_route = jnp.zeros((d, ROUTE_LANES), F32).at[:, :N_GROUPS].set(rg_w).at[:, N_GROUPS:N_GROUPS + N_EXPERTS].set(re_w)
    b_route = jnp.zeros((1, ROUTE_LANES), F32).at[0, :N_GROUPS].set(rg_b).at[0, N_GROUPS:N_GROUPS + N_EXPERTS].set(re_b)
    hb, route = moe_router(x, norm_g, w_route, b_route)
    experts = route[:, 0:2].astype(jnp.int32)
    gates = route[:, 2:4]
    k_sel = experts.shape[1]
    n_asg = t * k_sel
    flat_e = experts.reshape(n_asg)
    onehot = (flat_e[:, None] == jnp.arange(N_EXPERTS, dtype=jnp.int32)[None, :]).astype(jnp.int32)
    csum = jnp.cumsum(onehot, axis=0)
    rank = jnp.take_along_axis(csum, flat_e[:, None], axis=1)[:, 0] - 1
    counts = csum[-1]
    starts = jnp.cumsum(counts) - counts
    padded = (counts + tb - 1) // tb * tb
    pad_end = jnp.cumsum(padded)
    pad_start = pad_end - padded
    dest = pad_start[flat_e] + rank
    n_blk = (n_asg + N_EXPERTS * (tb - 1) + tb - 1) // tb
    order = jnp.argsort(flat_e).astype(jnp.int32)
    r = jnp.arange(n_blk * tb, dtype=jnp.int32)
    row_e = jnp.minimum(jnp.searchsorted(pad_end, r, side='right'), N_EXPERTS - 1).astype(jnp.int32)
    within = r - pad_start[row_e]
    src = jnp.clip(starts[row_e] + within, 0, n_asg - 1)
    row_tok = jnp.where(within < counts[row_e], order[src] // k_sel, 0)
    blk_e = row_e[::tb]
    n_used = (pad_end[-1:] // tb).astype(jnp.int32)
    yb = moe_experts(hb[row_tok], blk_e, n_used, w_gate, w_up, w_down, layer, tb)
    y = jnp.sum(yb[dest].reshape(t, k_sel, d) * gates[:, :, None], axis=1)
    return x + y


def _lambda_init(layer):
    return 0.8 - 0.6 * math.exp(-0.3 * layer)


def _permute_even_in(w_in):
    d = w_in.shape[0]
    lr0 = 2 * GLA_KEY_W + 2 * GLA_VAL_W
    return jnp.concatenate([w_in[:, :lr0], w_in[:, lr0 + GLA_RANK:], w_in[:, lr0:lr0 + GLA_RANK],
                            jnp.zeros((d, LANES - GLA_RANK), w_in.dtype)], axis=1).astype(BF16)


def kernel(x_prompt, x_sample, cache_k, cache_v, state_gla, state_conv, page_table, p_prompt, p_sample, rel_bias, norm_mix, norm_ffn, norm_ple, w_in_ev, w_lr_up, b_lr, gla_norm, conv_w, conv_b, conv_ln_g, conv_ln_b, w_out_ev, w_qkv, q_norm, k_norm, lam_q1, lam_k1, lam_q2, lam_k2, subln, w_out_od, router_g_w, router_g_b, router_e_w, router_e_b, moe_w_gate, moe_w_up, moe_w_down, ple_proj, ple_gate):
    n_bp, seq, d = x_prompt.shape
    n_bs = x_sample.shape[0]
    depth = norm_mix.shape[0]
    n_layers_odd, n_pool, page = cache_k.shape[:3]
    xp = x_prompt.reshape(n_bp * seq, d)
    xs = x_sample.reshape(n_bs, d)
    ck = cache_k.reshape(n_layers_odd, n_pool, page, d)
    cv = cache_v.reshape(n_layers_odd, n_pool, page, d)
    step_rows = 16
    kp_l, vp_l, ks_l, vs_l, gp_l, gs_l, cp_l, cs_l = [], [], [], [], [], [], [], []
    for i in range(depth):
        j = i // 2
        if i % 2 == 0:
            w_in = _permute_even_in(w_in_ev[j])
            w_lr_pad = jnp.zeros((LANES, GLA_KEY_W), BF16).at[:GLA_RANK].set(w_lr_up[j].astype(BF16))
            cw_pad = jnp.zeros((CONV_HALO, CONV_CH), F32).at[:CONV_WIDTH].set(conv_w[j])
            w_out = w_out_ev[j].astype(BF16)
            conv_args = (cw_pad, conv_b[j], conv_ln_g[j], conv_ln_b[j])
            zp = norm_proj(xp, norm_mix[i], w_in)
            op, g_p = gla_mix(zp, n_bp, seq, GLA_CHUNK, 512, w_lr_pad, b_lr[j], gla_norm[j], None)
            cp, c_p = conv_mix(zp, n_bp, seq, *conv_args)
            xp = proj_residual([op, cp], w_out, xp)
            zs = norm_proj(xs, norm_mix[i], w_in)
            zs_pad = jnp.zeros((n_bs, step_rows, EVEN_Z_W), F32).at[:, 0].set(zs).reshape(n_bs * step_rows, EVEN_Z_W)
            os_pad, g_s = gla_mix(zs_pad, n_bs, step_rows, step_rows, step_rows, w_lr_pad, b_lr[j], gla_norm[j],
                                  state_gla[j].reshape(n_bs, GLA_KEY_W, GLA_DV), valid_rows=1)
            os_ = os_pad.reshape(n_bs, step_rows, GLA_VAL_W)[:, 0]
            cs, c_s = conv_step(zs, state_conv[j], *conv_args)
            xs = proj_residual([os_, cs], w_out, xs)
            gp_l.append(g_p.reshape(n_bp, GLA_HEADS, GLA_DK, GLA_DV))
            gs_l.append(g_s.reshape(n_bs, GLA_HEADS, GLA_DK, GLA_DV))
            cp_l.append(c_p)
            cs_l.append(c_s)
        else:
            lam0 = _lambda_init(i)
            lamv = jnp.stack([lam_q1[j], lam_k1[j], lam_q2[j], lam_k2[j]]).astype(F32)
            wq = w_qkv[j].astype(BF16)
            w_out = w_out_od[j].astype(BF16)
            qp, kp, vp, kpb, vpb = qkv_proj(xp, norm_mix[i], wq, q_norm[j], k_norm[j])
            ap = diff_attention(qp, kpb, vpb, n_bp, seq, rel_bias, lamv, subln[j], lam0)
            xp = proj_residual([ap], w_out, xp)
            qs, ks, vs, _, _ = qkv_proj(xs, norm_mix[i], wq, q_norm[j], k_norm[j])
            as_ = diff_attention_step(qs, ks, vs, ck, cv, j, page_table, rel_bias, lamv, subln[j], lam0)
            xs = proj_residual([as_], w_out, xs)
            kp_l.append(kp.reshape(n_bp, seq, DIFF_HEADS, 2, DIFF_DH))
            vp_l.append(vp.reshape(n_bp, seq, DIFF_HEADS, DIFF_DV))
            ks_l.append(ks.reshape(n_bs, 1, DIFF_HEADS, 2, DIFF_DH))
            vs_l.append(vs.reshape(n_bs, 1, DIFF_HEADS, DIFF_DV))
        moe_args = (i, norm_ffn[i], router_g_w[i], router_g_b[i], router_e_w[i], router_e_b[i],
                    moe_w_gate, moe_w_up, moe_w_down)
        xp = hier_moe_add(xp, *moe_args)
        xs = hier_moe_add(xs, *moe_args)
        wg = ple_gate[i].astype(BF16)
        wp = ple_proj[i].astype(BF16)
        xp = ple_add(xp, p_prompt[i].reshape(n_bp * seq, PLE_DIM), norm_ple[i], wg, wp)
        xs = ple_add(xs, p_sample[i].reshape(n_bs, PLE_DIM), norm_ple[i], wg, wp)
    return (xp.reshape(n_bp, seq, d), xs.reshape(n_bs, 1, d),
            jnp.stack(kp_l), jnp.stack(vp_l), jnp.stack(ks_l), jnp.stack(vs_l),
            jnp.stack(gp_l), jnp.stack(gs_l), jnp.stack(cp_l), jnp.stack(cs_l))
```

```python
import functools
import math

import numpy as np
import jax
import jax.numpy as jnp
from jax import lax
from jax.experimental import pallas as pl
from jax.experimental.pallas import tpu as pltpu

F32 = jnp.float32
BF16 = jnp.bfloat16

D_MODEL = 1024
NORM_EPS = 1e-6
GLA_HEADS = 4
GLA_DK = 64
GLA_DV = 128
GLA_KEY_W = GLA_HEADS * GLA_DK
GLA_VAL_W = GLA_HEADS * GLA_DV
GLA_RANK = 16
GLA_TAU = 16.0
GLA_CHUNK = 64
CONV_CH = 512
CONV_WIDTH = 31
CONV_HALO = 32
DIFF_HEADS = 8
DIFF_DH = 64
DIFF_DV = 128
REL_BUCKETS = 32
REL_MAX_DIST = 128
ATT_TILE = 512
MASK_VALUE = -1e30
N_GROUPS = 4
EXPERTS_PER_GROUP = 8
N_EXPERTS = N_GROUPS * EXPERTS_PER_GROUP
D_FF = 512
MOE_TILE = 256
ROUTE_LANES = 128
PLE_DIM = 256

LANES = 128
VMEM_LIMIT_BYTES = 56 * 1024 * 1024
EVEN_Z_W = 2 * GLA_KEY_W + 2 * GLA_VAL_W + 2 * CONV_CH + LANES
EVEN_LR_BLOCK = (2 * GLA_KEY_W + 2 * GLA_VAL_W + 2 * CONV_CH) // LANES


def _params(*sem):
    return pltpu.CompilerParams(dimension_semantics=sem, vmem_limit_bytes=VMEM_LIMIT_BYTES)


def _dot(a, b):
    return jnp.dot(a, b, preferred_element_type=F32)


def _dot_nt(a, b):
    return lax.dot_general(a, b, (((1,), (1,)), ((), ())), preferred_element_type=F32)


def _dot_tn(a, b):
    return lax.dot_general(a, b, (((0,), (0,)), ((), ())), preferred_element_type=F32)


def _split_bf16(x):
    hi = x.astype(BF16)
    lo = (x - hi.astype(F32)).astype(BF16)
    return hi, lo


def _rmsnorm(x, g):
    ms = jnp.mean(x * x, axis=-1, keepdims=True)
    return x * lax.rsqrt(ms + NORM_EPS) * g


def _sigmoid(x):
    return 1.0 / (1.0 + jnp.exp(-x))


def _silu(x):
    return x * _sigmoid(x)


def _row_tile(n_rows, want):
    t = min(want, n_rows)
    assert n_rows % t == 0, (n_rows, t)
    return t


def _norm_proj_kernel(x_ref, g_ref, w_ref, o_ref):
    h = _rmsnorm(x_ref[...], g_ref[...]).astype(BF16)
    o_ref[...] = _dot(h, w_ref[...])


def norm_proj(x, g, w_bf16, tm=256):
    t, d = x.shape
    n = w_bf16.shape[1]
    tm = _row_tile(t, tm)
    return pl.pallas_call(
        _norm_proj_kernel,
        out_shape=jax.ShapeDtypeStruct((t, n), F32),
        grid=(t // tm,),
        in_specs=[pl.BlockSpec((tm, d), lambda i: (i, 0)),
                  pl.BlockSpec((1, d), lambda i: (0, 0)),
                  pl.BlockSpec((d, n), lambda i: (0, 0))],
        out_specs=pl.BlockSpec((tm, n), lambda i: (i, 0)),
        compiler_params=_params("parallel"),
        name="norm_proj",
    )(x, g.reshape(1, d), w_bf16)


def _proj_residual_kernel(*refs, n_in):
    a_refs = refs[:n_in]
    w_ref, x_ref, o_ref = refs[n_in:]
    acc = x_ref[...]
    row = 0
    for a_ref in a_refs:
        k = a_ref.shape[1]
        acc = acc + _dot(a_ref[...].astype(BF16), w_ref[row:row + k, :])
        row += k
    o_ref[...] = acc


def proj_residual(acts, w_bf16, x, tm=512):
    t, d = x.shape
    tm = _row_tile(t, tm)
    n_in = len(acts)
    in_specs = [pl.BlockSpec((tm, a.shape[1]), lambda i: (i, 0)) for a in acts]
    in_specs += [pl.BlockSpec(w_bf16.shape, lambda i: (0, 0)),
                 pl.BlockSpec((tm, d), lambda i: (i, 0))]
    return pl.pallas_call(
        functools.partial(_proj_residual_kernel, n_in=n_in),
        out_shape=jax.ShapeDtypeStruct((t, d), F32),
        grid=(t // tm,),
        in_specs=in_specs,
        out_specs=pl.BlockSpec((tm, d), lambda i: (i, 0)),
        compiler_params=_params("parallel"),
        name="proj_residual",
    )(*acts, w_bf16, x)


def _ple_kernel(x_ref, p_ref, g_ref, wg_ref, wp_ref, o_ref):
    x = x_ref[...]
    h = _rmsnorm(x, g_ref[...]).astype(BF16)
    gate = _sigmoid(_dot(h, wg_ref[...]))
    proj = _dot(p_ref[...].astype(BF16), wp_ref[...])
    o_ref[...] = x + gate * proj


def ple_add(x, p, g, wg_bf16, wp_bf16, tm=512):
    t, d = x.shape
    tm = _row_tile(t, tm)
    pd = p.shape[1]
    return pl.pallas_call(
        _ple_kernel,
        out_shape=jax.ShapeDtypeStruct((t, d), F32),
        grid=(t // tm,),
        in_specs=[pl.BlockSpec((tm, d), lambda i: (i, 0)),
                  pl.BlockSpec((tm, pd), lambda i: (i, 0)),
                  pl.BlockSpec((1, d), lambda i: (0, 0)),
                  pl.BlockSpec((d, d), lambda i: (0, 0)),
                  pl.BlockSpec((pd, d), lambda i: (0, 0))],
        out_specs=pl.BlockSpec((tm, d), lambda i: (i, 0)),
        compiler_params=_params("parallel"),
        name="ple_add",
    )(x, p, g.reshape(1, d), wg_bf16, wp_bf16)


def _gla_constants(chunk):
    c = chunk
    t = np.arange(c)
    tri = (t[None, :] <= t[:, None]).astype(np.float32)
    causal4 = np.tile(tri, (GLA_HEADS, 1))
    lane_head = np.arange(GLA_KEY_W) // GLA_DK
    row_head = np.repeat(np.arange(GLA_HEADS), c)
    headmask4 = (row_head[:, None] == lane_head[None, :]).astype(np.float32)
    blockdiag = (np.arange(GLA_KEY_W)[:, None] // GLA_DK
                 == np.arange(GLA_VAL_W)[None, :] // GLA_DV).astype(np.float32)
    return (jnp.asarray(tri, BF16), jnp.asarray(causal4), jnp.asarray(headmask4), jnp.asarray(blockdiag))


def _gla_kernel(*refs, chunk, n_chunks, has_state, valid_rows):
    if has_state:
        (qk_ref, v_ref, r_ref, lr_ref, wlr_ref, blr_ref, gg_ref, tri_ref, causal_ref, hmask_ref, bd_ref,
         s0_ref, o_ref, sout_ref, s_scr) = refs
    else:
        (qk_ref, v_ref, r_ref, lr_ref, wlr_ref, blr_ref, gg_ref, tri_ref, causal_ref, hmask_ref, bd_ref,
         o_ref, sout_ref, s_scr) = refs
    c = chunk
    i = pl.program_id(1)
    bd = bd_ref[...]

    @pl.when(i == 0)
    def _init():
        if has_state:
            s0 = s0_ref[0]
            s_scr[...] = jnp.concatenate([s0] * GLA_HEADS, axis=1) * bd
        else:
            s_scr[...] = jnp.zeros_like(s_scr)

    tri = tri_ref[...]
    ones = jnp.ones((c, LANES), BF16)

    def chunk_step(ci, carry):
        rows = pl.ds(pl.multiple_of(ci * c, c), c)
        qk = qk_ref[rows, :]
        q = qk[:, :GLA_KEY_W] * (GLA_DK ** -0.5)
        k = qk[:, GLA_KEY_W:]
        vb = v_ref[rows, :].astype(BF16)
        x = _dot(lr_ref[rows, :].astype(BF16), wlr_ref[...]) + blr_ref[...]
        g = (jnp.minimum(x, 0.0) - jnp.log1p(jnp.exp(-jnp.abs(x)))) * (1.0 / GLA_TAU)
        if valid_rows < c:
            g = jnp.where(lax.broadcasted_iota(jnp.int32, g.shape, 0) < valid_rows, g, 0.0)
        g_hi, g_lo = _split_bf16(g)
        b = _dot(tri, g_hi) + _dot(tri, g_lo)
        dcol = _dot_tn(g_hi, ones) + _dot_tn(g_lo, ones)
        bend = b[c - 1:c, :]
        half = 0.5 * bend
        qs = q * jnp.exp(b - half)
        ks = k * jnp.exp(half - b)
        qt = q * jnp.exp(b)
        kh = k * jnp.exp(bend - b)
        qstack = (jnp.concatenate([qs] * GLA_HEADS, axis=0) * hmask_ref[...]).astype(BF16)
        scores = (_dot_nt(qstack, ks.astype(BF16)) * causal_ref[...]).astype(BF16)
        s_bd = s_scr[...]
        o_inter = _dot(qt.astype(BF16), s_bd.astype(BF16))
        gg = gg_ref[...]
        for h in range(GLA_HEADS):
            cols = slice(h * GLA_DV, (h + 1) * GLA_DV)
            o_h = _dot(scores[h * c:(h + 1) * c, :], vb[:, cols]) + o_inter[:, cols]
            y = _rmsnorm(o_h, gg) * _silu(r_ref[rows, cols])
            o_ref[rows, cols] = y.astype(o_ref.dtype)
        decay = jnp.concatenate([jnp.exp(dcol)] * GLA_HEADS, axis=1)
        s_scr[...] = s_bd * decay + _dot_tn(kh.astype(BF16), vb) * bd
        return carry

    lax.fori_loop(0, n_chunks, chunk_step, 0)

    @pl.when(i == pl.num_programs(1) - 1)
    def _final():
        s_bd = s_scr[...]
        for h in range(GLA_HEADS):
            sout_ref[0, h * GLA_DK:(h + 1) * GLA_DK, :] = s_bd[h * GLA_DK:(h + 1) * GLA_DK,
                                                               h * GLA_DV:(h + 1) * GLA_DV]


def gla_mix(z, n_batch, seq, chunk, tile, w_lr_pad, b_lr, gla_g, state0, valid_rows=None):
    nt = seq // tile
    assert seq % tile == 0 and tile % chunk == 0
    valid_rows = chunk if valid_rows is None else valid_rows
    assert valid_rows == chunk or seq == chunk
    has_state = state0 is not None
    consts = _gla_constants(chunk)
    row = lambda b, i: b * nt + i
    in_specs = [pl.BlockSpec((tile, 2 * GLA_KEY_W), lambda b, i: (row(b, i), 0)),
                pl.BlockSpec((tile, GLA_VAL_W), lambda b, i: (row(b, i), 1)),
                pl.BlockSpec((tile, GLA_VAL_W), lambda b, i: (row(b, i), 2)),
                pl.BlockSpec((tile, LANES), lambda b, i: (row(b, i), EVEN_LR_BLOCK)),
                pl.BlockSpec(w_lr_pad.shape, lambda b, i: (0, 0)),
                pl.BlockSpec((1, GLA_KEY_W), lambda b, i: (0, 0)),
                pl.BlockSpec((1, GLA_DV), lambda b, i: (0, 0))]
    in_specs += [pl.BlockSpec(a.shape, lambda b, i: (0, 0)) for a in consts]
    args = [z, z, z, z, w_lr_pad, b_lr.reshape(1, GLA_KEY_W), gla_g.reshape(1, GLA_DV), *consts]
    if has_state:
        in_specs.append(pl.BlockSpec((1, GLA_KEY_W, GLA_DV), lambda b, i: (b, 0, 0)))
        args.append(state0)
    return pl.pallas_call(
        functools.partial(_gla_kernel, chunk=chunk, n_chunks=tile // chunk, has_state=has_state,
                          valid_rows=valid_rows),
        out_shape=(jax.ShapeDtypeStruct((n_batch * seq, GLA_VAL_W), BF16),
                   jax.ShapeDtypeStruct((n_batch, GLA_KEY_W, GLA_DV), F32)),
        grid=(n_batch, nt),
        in_specs=in_specs,
        out_specs=(pl.BlockSpec((tile, GLA_VAL_W), lambda b, i: (row(b, i), 0)),
                   pl.BlockSpec((1, GLA_KEY_W, GLA_DV), lambda b, i: (b, 0, 0))),
        scratch_shapes=[pltpu.VMEM((GLA_KEY_W, GLA_VAL_W), F32)],
        compiler_params=_params("parallel", "arbitrary"),
        name="gla_mix",
    )(*args)


def _layernorm_silu(c, g, b):
    mu = jnp.mean(c, axis=-1, keepdims=True)
    cc = c - mu
    var = jnp.mean(cc * cc, axis=-1, keepdims=True)
    return _silu(cc * lax.rsqrt(var + NORM_EPS) * g + b)


def _conv_kernel(a_ref, gt_ref, cw_ref, cb_ref, lg_ref, lb_ref, o_ref, st_ref, ext, *, tile, sub):
    i = pl.program_id(1)
    n_state = CONV_WIDTH - 1

    @pl.when(i == 0)
    def _zero_halo():
        ext[0:CONV_HALO, :] = jnp.zeros((CONV_HALO, CONV_CH), F32)

    @pl.when(i > 0)
    def _carry_halo():
        ext[0:CONV_HALO, :] = ext[tile:tile + CONV_HALO, :]

    ext[CONV_HALO:CONV_HALO + tile, :] = a_ref[...] * _sigmoid(gt_ref[...])
    lg = lg_ref[...]
    lb = lb_ref[...]
    for s in range(tile // sub):
        acc = jnp.broadcast_to(cb_ref[...], (sub, CONV_CH))
        for w in range(CONV_WIDTH):
            off = s * sub + (CONV_HALO - n_state) + w
            acc = acc + ext[off:off + sub, :] * cw_ref[w:w + 1, :]
        o_ref[s * sub:(s + 1) * sub, :] = _layernorm_silu(acc, lg, lb).astype(o_ref.dtype)

    @pl.when(i == pl.num_programs(1) - 1)
    def _state():
        st_ref[0] = ext[tile + CONV_HALO - n_state:tile + CONV_HALO, :]


def conv_mix(z, n_batch, seq, conv_w_pad, conv_b, ln_g, ln_b, tile=256, sub=32):
    nt = seq // tile
    assert seq % tile == 0 and tile % sub == 0 and tile >= CONV_HALO
    row = lambda b, i: b * nt + i
    vec = lambda a: a.reshape(1, CONV_CH)
    return pl.pallas_call(
        functools.partial(_conv_kernel, tile=tile, sub=sub),
        out_shape=(jax.ShapeDtypeStruct((n_batch * seq, CONV_CH), BF16),
                   jax.ShapeDtypeStruct((n_batch, CONV_WIDTH - 1, CONV_CH), F32)),
        grid=(n_batch, nt),
        in_specs=[pl.BlockSpec((tile, CONV_CH), lambda b, i: (row(b, i), 3)),
                  pl.BlockSpec((tile, CONV_CH), lambda b, i: (row(b, i), 4)),
                  pl.BlockSpec(conv_w_pad.shape, lambda b, i: (0, 0)),
                  pl.BlockSpec((1, CONV_CH), lambda b, i: (0, 0)),
                  pl.BlockSpec((1, CONV_CH), lambda b, i: (0, 0)),
                  pl.BlockSpec((1, CONV_CH), lambda b, i: (0, 0))],
        out_specs=(pl.BlockSpec((tile, CONV_CH), lambda b, i: (row(b, i), 0)),
                   pl.BlockSpec((1, CONV_WIDTH - 1, CONV_CH), lambda b, i: (b, 0, 0))),
        scratch_shapes=[pltpu.VMEM((tile + CONV_HALO, CONV_CH), F32)],
        compiler_params=_params("parallel", "arbitrary"),
        name="conv_mix",
    )(z, z, conv_w_pad, vec(conv_b), vec(ln_g), vec(ln_b))


def _conv_step_kernel(a_ref, gt_ref, st_ref, cw_ref, cb_ref, lg_ref, lb_ref, o_ref, sto_ref):
    n_state = CONV_WIDTH - 1
    u = a_ref[...] * _sigmoid(gt_ref[...])
    acc = cb_ref[...] + u * cw_ref[n_state:n_state + 1, :]
    for w in range(n_state):
        acc = acc + st_ref[:, w * CONV_CH:(w + 1) * CONV_CH] * cw_ref[w:w + 1, :]
    o_ref[...] = _layernorm_silu(acc, lg_ref[...], lb_ref[...]).astype(o_ref.dtype)
    sto_ref[:, 0:(n_state - 1) * CONV_CH] = st_ref[:, CONV_CH:n_state * CONV_CH]
    sto_ref[:, (n_state - 1) * CONV_CH:] = u


def conv_step(z, state, conv_w_pad, conv_b, ln_g, ln_b, tile=32):
    nb = z.shape[0]
    n_state = CONV_WIDTH - 1
    tile = _row_tile(nb, tile)
    vec = lambda a: a.reshape(1, CONV_CH)
    st2 = state.reshape(nb, n_state * CONV_CH)
    out, st_new = pl.pallas_call(
        _conv_step_kernel,
        out_shape=(jax.ShapeDtypeStruct((nb, CONV_CH), BF16),
                   jax.ShapeDtypeStruct((nb, n_state * CONV_CH), F32)),
        grid=(nb // tile,),
        in_specs=[pl.BlockSpec((tile, CONV_CH), lambda i: (i, 3)),
                  pl.BlockSpec((tile, CONV_CH), lambda i: (i, 4)),
                  pl.BlockSpec((tile, n_state * CONV_CH), lambda i: (i, 0)),
                  pl.BlockSpec(conv_w_pad.shape, lambda i: (0, 0)),
                  pl.BlockSpec((1, CONV_CH), lambda i: (0, 0)),
                  pl.BlockSpec((1, CONV_CH), lambda i: (0, 0)),
                  pl.BlockSpec((1, CONV_CH), lambda i: (0, 0))],
        out_specs=(pl.BlockSpec((tile, CONV_CH), lambda i: (i, 0)),
                   pl.BlockSpec((tile, n_state * CONV_CH), lambda i: (i, 0))),
        compiler_params=_params("parallel"),
        name="conv_step",
    )(z, z, st2, conv_w_pad, vec(conv_b), vec(ln_g), vec(ln_b))
    return out, st_new.reshape(nb, n_state, CONV_CH)


def _group_constants():
    grp = np.zeros((D_MODEL, LANES), np.float32)
    grp[np.arange(D_MODEL), np.arange(D_MODEL) // DIFF_DH] = 1.0
    return jnp.asarray(grp, BF16), jnp.asarray(grp.T.copy(), BF16)


def _qkv_kernel(x_ref, g_ref, w_ref, grp_ref, grpt_ref, qg_ref, kg_ref, q_ref, k_ref, v_ref, kb_ref, vb_ref):
    h = _rmsnorm(x_ref[...], g_ref[...]).astype(BF16)
    z = _dot(h, w_ref[...])
    grp = grp_ref[...]
    grpt = grpt_ref[...]

    def map_norm(y, gain):
        hi, lo = _split_bf16(y * y)
        ms = (_dot(hi, grp) + _dot(lo, grp)) * (1.0 / DIFF_DH)
        rh, rl = _split_bf16(lax.rsqrt(ms + NORM_EPS))
        return y * (_dot(rh, grpt) + _dot(rl, grpt)) * gain

    qn = map_norm(z[:, :D_MODEL], qg_ref[...])
    kn = map_norm(z[:, D_MODEL:2 * D_MODEL], kg_ref[...])
    v = z[:, 2 * D_MODEL:]
    q_ref[...] = (qn * (DIFF_DH ** -0.5)).astype(BF16)
    k_ref[...] = kn
    v_ref[...] = v
    kb_ref[...] = kn.astype(BF16)
    ones = jnp.ones((v.shape[0], DIFF_DV), BF16)
    for h in range(DIFF_HEADS):
        vb_ref[:, 2 * h * DIFF_DV:(2 * h + 1) * DIFF_DV] = v[:, h * DIFF_DV:(h + 1) * DIFF_DV].astype(BF16)
        vb_ref[:, (2 * h + 1) * DIFF_DV:(2 * h + 2) * DIFF_DV] = ones


def qkv_proj(x, g, w_bf16, q_gain, k_gain, tm=256):
    t, d = x.shape
    tm = _row_tile(t, tm)
    grp, grpt = _group_constants()
    tile_gain = lambda a: jnp.tile(a, D_MODEL // DIFF_DH).reshape(1, D_MODEL)
    row = pl.BlockSpec((tm, d), lambda i: (i, 0))
    full = lambda a: pl.BlockSpec(a.shape, lambda i: (0, 0))
    qg, kg = tile_gain(q_gain), tile_gain(k_gain)
    return pl.pallas_call(
        _qkv_kernel,
        out_shape=(jax.ShapeDtypeStruct((t, d), BF16), jax.ShapeDtypeStruct((t, d), F32),
                   jax.ShapeDtypeStruct((t, d), F32), jax.ShapeDtypeStruct((t, d), BF16),
                   jax.ShapeDtypeStruct((t, 2 * d), BF16)),
        grid=(t // tm,),
        in_specs=[row, pl.BlockSpec((1, d), lambda i: (0, 0)), full(w_bf16), full(grp), full(grpt),
                  full(qg), full(kg)],
        out_specs=(row, row, row, row, pl.BlockSpec((tm, 2 * d), lambda i: (i, 0))),
        compiler_params=_params("parallel"),
        name="qkv_proj",
    )(x, g.reshape(1, d), w_bf16, grp, grpt, qg, kg)


def _rel_bucket_np(n):
    max_exact = REL_BUCKETS // 2
    nf = np.maximum(n, 1).astype(np.float64)
    large = max_exact + (np.log(nf / max_exact) / math.log(REL_MAX_DIST / max_exact)
                         * (REL_BUCKETS - max_exact)).astype(np.int32)
    return np.where(n < max_exact, np.maximum(n, 0), np.minimum(large, REL_BUCKETS - 1))


def _diff_lambda(lamv, lam0):
    s1 = jnp.sum(lamv[0:1, :] * lamv[1:2, :], axis=-1, keepdims=True)
    s2 = jnp.sum(lamv[2:3, :] * lamv[3:4, :], axis=-1, keepdims=True)
    return jnp.exp(s1) - jnp.exp(s2) + lam0


def _attn_kernel(q_ref, k_ref, v_ref, bdiag_ref, bsub_ref, bfar_ref, lamv_ref, sub_ref, o_ref,
                 m_scr, acc_scr, *, tile, lam0):
    qi = pl.program_id(2)
    q = q_ref[...]
    lane = lax.broadcasted_iota(jnp.int32, q.shape, 1)
    zero = jnp.zeros_like(q)
    q_maps = (jnp.where(lane < DIFF_DH, q, zero), jnp.where(lane >= DIFF_DH, q, zero))
    reps = tile // LANES

    m_scr[...] = jnp.full(m_scr.shape, -jnp.inf, F32)
    acc_scr[...] = jnp.zeros(acc_scr.shape, F32)

    def flash_step(kt, bias):
        rows = pl.ds(pl.multiple_of(kt * tile, tile), tile)
        k = k_ref[rows, :]
        v = v_ref[rows, :]
        for m in range(2):
            s = _dot_nt(q_maps[m], k) + bias
            m_prev = m_scr[m]
            m_new = jnp.maximum(m_prev, jnp.max(s, axis=1, keepdims=True))
            alpha = jnp.exp(m_prev - m_new)
            p = jnp.exp(s - jnp.concatenate([m_new] * reps, axis=1))
            acc_scr[m] = jnp.concatenate([alpha, alpha], axis=1) * acc_scr[m] + _dot(p.astype(BF16), v)
            m_scr[m] = m_new

    def far_step(kt, carry):
        flash_step(kt, bfar_ref[0])
        return carry

    lax.fori_loop(0, jnp.maximum(qi - 1, 0), far_step, 0)

    @pl.when(qi >= 1)
    def _sub():
        flash_step(qi - 1, bsub_ref[0])

    flash_step(qi, bdiag_ref[0])

    lam = _diff_lambda(lamv_ref[...], lam0)
    a0 = acc_scr[0]
    a1 = acc_scr[1]
    o = a0[:, :DIFF_DV] / a0[:, DIFF_DV:] - lam * (a1[:, :DIFF_DV] / a1[:, DIFF_DV:])
    o_ref[...] = (_rmsnorm(o, sub_ref[...]) * (1.0 - lam0)).astype(o_ref.dtype)


def _bucket_lookup(rel_bias, n):
    onehot = np.zeros((n.size, REL_BUCKETS), np.float32)
    onehot[np.arange(n.size), _rel_bucket_np(n).reshape(-1)] = 1.0
    out = jnp.dot(jnp.asarray(onehot), rel_bias.astype(F32), precision=lax.Precision.HIGHEST)
    return out.reshape(n.shape + (rel_bias.shape[1],))


def prompt_bias_tables(rel_bias, tile=ATT_TILE):
    r = np.arange(tile)
    n_diag = r[:, None] - r[None, :]
    bdiag = jnp.where(jnp.asarray(n_diag >= 0)[None], jnp.moveaxis(_bucket_lookup(rel_bias, n_diag), -1, 0),
                      MASK_VALUE)
    bsub = jnp.moveaxis(_bucket_lookup(rel_bias, n_diag + tile), -1, 0)
    assert tile + 1 >= REL_MAX_DIST
    bfar = jnp.broadcast_to(rel_bias[REL_BUCKETS - 1][:, None, None], (DIFF_HEADS, 1, tile))
    return bdiag.astype(F32), bsub.astype(F32), bfar.astype(F32)


def diff_attention(q, kb, vb1, n_batch, seq, bias_tables, lamv, subln, lam0, tile=ATT_TILE):
    nq = seq // tile
    assert seq % tile == 0
    bdiag, bsub, bfar = bias_tables
    head_tile = lambda b, h, i: (h, 0, 0)
    return pl.pallas_call(
        functools.partial(_attn_kernel, tile=tile, lam0=lam0),
        out_shape=jax.ShapeDtypeStruct((n_batch * seq, D_MODEL), BF16),
        grid=(n_batch, DIFF_HEADS, nq),
        in_specs=[pl.BlockSpec((tile, DIFF_DV), lambda b, h, i: (b * nq + i, h)),
                  pl.BlockSpec((seq, DIFF_DV), lambda b, h, i: (b, h)),
                  pl.BlockSpec((seq, 2 * DIFF_DV), lambda b, h, i: (b, h)),
                  pl.BlockSpec((1, tile, tile), head_tile),
                  pl.BlockSpec((1, tile, tile), head_tile),
                  pl.BlockSpec((1, 1, tile), head_tile),
                  pl.BlockSpec((4, DIFF_DH), lambda b, h, i: (0, 0)),
                  pl.BlockSpec((1, DIFF_DV), lambda b, h, i: (0, 0))],
        out_specs=pl.BlockSpec((tile, DIFF_DV), lambda b, h, i: (b * nq + i, h)),
        scratch_shapes=[pltpu.VMEM((2, tile, LANES), F32), pltpu.VMEM((2, tile, 2 * DIFF_DV), F32)],
        compiler_params=_params("parallel", "parallel", "arbitrary"),
        name="diff_attention",
    )(q, kb, vb1, bdiag, bsub, bfar, lamv, subln.reshape(1, DIFF_DV))


N_MAPS = 2 * DIFF_HEADS


def step_bias_tables(rel_bias, n_past):
    n = n_past - np.arange(n_past)
    past = _bucket_lookup(rel_bias, n).T
    new = jnp.broadcast_to(_bucket_lookup(rel_bias, np.zeros((1,), np.int64)).T, (DIFF_HEADS, LANES))
    return jnp.concatenate([past, past], axis=0), jnp.concatenate([new, new], axis=0)


def _step_constants(page):
    j = np.arange(N_MAPS)
    c = np.arange(D_MODEL)
    col_head, col_map = c // DIFF_DV, (c % DIFF_DV) // DIFF_DH
    sel = ((j[:, None] % DIFF_HEADS == col_head[None, :])
           & (j[:, None] // DIFF_HEADS == col_map[None, :])).astype(np.float32)
    r = np.arange(page * DIFF_HEADS)
    tok_of_row = (r[:, None] // DIFF_HEADS == np.arange(page)[None, :]).astype(np.float32)
    return jnp.asarray(sel), jnp.asarray(tok_of_row)


def _attn_step_kernel(pt_ref, q_ref, kn_ref, vn_ref, *refs, n_pages, page, lam0):
    kt_refs = refs[:n_pages]
    v_refs = refs[n_pages:2 * n_pages]
    (bias_ref, nbias_ref, sel_ref, tok_ref, lamv_ref, sub_ref, o_ref, s_scr) = refs[2 * n_pages:]
    sel = sel_ref[...]
    q_sel = sel * q_ref[0].astype(F32)
    q_bf = q_sel.astype(BF16)
    for p in range(n_pages):
        cols = slice(p * page, (p + 1) * page)
        s_scr[:, cols] = _dot(q_bf, kt_refs[p][...].astype(BF16)) + bias_ref[:, cols]
    s_new = jnp.sum(q_sel * kn_ref[0], axis=1, keepdims=True) + nbias_ref[:, 0:1]
    s = s_scr[...]
    m = jnp.maximum(jnp.max(s, axis=1, keepdims=True), s_new)
    p_all = jnp.exp(s - m)
    p_new = jnp.exp(s_new - m)
    l = jnp.sum(p_all, axis=1, keepdims=True) + p_new
    lam = _diff_lambda(lamv_ref[...], lam0)
    row = lax.broadcasted_iota(jnp.int32, (N_MAPS, 1), 0)
    coef = jnp.where(row < DIFF_HEADS, 1.0 / l, -lam / l)
    w_all = p_all * coef
    w_heads = w_all[:DIFF_HEADS, :] + w_all[DIFF_HEADS:, :]
    w_new = (p_new * coef)[:DIFF_HEADS, :] + (p_new * coef)[DIFF_HEADS:, :]
    tok_of_row = tok_ref[...]
    acc = w_new * vn_ref[0]
    for p in range(n_pages):
        w_p = w_heads[:, p * page:(p + 1) * page]
        w_col = jnp.sum(pltpu.repeat(w_p, page, axis=0) * tok_of_row, axis=1, keepdims=True)
        acc = acc + jnp.sum((w_col * v_refs[p][...]).reshape(page, DIFF_HEADS, DIFF_DV), axis=0)
    o_ref[0] = _rmsnorm(acc, sub_ref[...]) * (1.0 - lam0)


def diff_attention_step(q, k_new, v_new, cache_kt, cache_v2, layer, page_table, bias_tables, lamv, subln, lam0):
    nb = q.shape[0]
    n_pages = page_table.shape[1]
    page = cache_kt.shape[3]
    n_past = n_pages * page
    past_bias, new_bias = bias_tables
    sel, tok_of_row = _step_constants(page)
    tok_spec = pl.BlockSpec((1, 1, D_MODEL), lambda b, pt: (b, 0, 0))
    head_spec = pl.BlockSpec((1, DIFF_HEADS, DIFF_DV), lambda b, pt: (b, 0, 0))
    kt_spec = lambda p: pl.BlockSpec((None, None, D_MODEL, page),
                                     lambda b, pt: (layer, pt[b * n_pages + p], 0, 0))
    v_spec = lambda p: pl.BlockSpec((None, None, page * DIFF_HEADS, DIFF_DV),
                                    lambda b, pt: (layer, pt[b * n_pages + p], 0, 0))
    full = lambda a: pl.BlockSpec(a.shape, lambda b, pt: (0,) * a.ndim)
    sub2 = subln.reshape(1, DIFF_DV)
    grid_spec = pltpu.PrefetchScalarGridSpec(
        num_scalar_prefetch=1,
        grid=(nb,),
        in_specs=([tok_spec, tok_spec, head_spec]
                  + [kt_spec(p) for p in range(n_pages)] + [v_spec(p) for p in range(n_pages)]
                  + [full(past_bias), full(new_bias), full(sel), full(tok_of_row), full(lamv), full(sub2)]),
        out_specs=head_spec,
        scratch_shapes=[pltpu.VMEM((N_MAPS, n_past), F32)],
    )
    out = pl.pallas_call(
        functools.partial(_attn_step_kernel, n_pages=n_pages, page=page, lam0=lam0),
        out_shape=jax.ShapeDtypeStruct((nb, DIFF_HEADS, DIFF_DV), F32),
        grid_spec=grid_spec,
        compiler_params=_params("parallel"),
        name="diff_attention_step",
    )(page_table.reshape(-1), q.reshape(nb, 1, D_MODEL), k_new.reshape(nb, 1, D_MODEL),
      v_new.reshape(nb, DIFF_HEADS, DIFF_DV),
      *([cache_kt] * n_pages), *([cache_v2] * n_pages), past_bias, new_bias, sel, tok_of_row, lamv, sub2)
    return out.reshape(nb, D_MODEL)


def _router_kernel(x_ref, g_ref, w_ref, b_ref, h_ref, r_ref):
    h = _rmsnorm(x_ref[...], g_ref[...])
    h_ref[...] = h
    logits = jnp.dot(h, w_ref[...], precision=lax.Precision.HIGHEST, preferred_element_type=F32) + b_ref[...]
    lane = lax.broadcasted_iota(jnp.int32, logits.shape, 1)
    neg = jnp.float32(-jnp.inf)
    big = jnp.int32(2 ** 30)

    def first_argmax(vals):
        mx = jnp.max(vals, axis=1, keepdims=True)
        idx = jnp.min(jnp.where(vals == mx, lane, big), axis=1, keepdims=True)
        return mx, idx

    lg = jnp.where(lane < N_GROUPS, logits, neg)
    g_max, grp = first_argmax(lg)
    w_grp = 1.0 / jnp.sum(jnp.exp(lg - g_max), axis=1, keepdims=True)
    e_lane = lane - N_GROUPS
    in_grp = (e_lane >= grp * EXPERTS_PER_GROUP) & (e_lane < (grp + 1) * EXPERTS_PER_GROUP)
    le = jnp.where(in_grp, logits, neg)
    m1, i1 = first_argmax(le)
    m2, i2 = first_argmax(jnp.where(lane == i1, neg, le))
    e2 = jnp.exp(m2 - m1)
    gate1 = w_grp / (1.0 + e2)
    gate2 = w_grp * e2 / (1.0 + e2)
    f = lambda a: a.astype(F32)
    r_ref[...] = jnp.where(lane == 0, f(i1 - N_GROUPS),
                           jnp.where(lane == 1, f(i2 - N_GROUPS),
                                     jnp.where(lane == 2, gate1, jnp.where(lane == 3, gate2, 0.0))))


def moe_router(x, g, w_route, b_route, tm=512):
    t, d = x.shape
    tm = _row_tile(t, tm)
    return pl.pallas_call(
        _router_kernel,
        out_shape=(jax.ShapeDtypeStruct((t, d), F32), jax.ShapeDtypeStruct((t, ROUTE_LANES), F32)),
        grid=(t // tm,),
        in_specs=[pl.BlockSpec((tm, d), lambda i: (i, 0)),
                  pl.BlockSpec((1, d), lambda i: (0, 0)),
                  pl.BlockSpec((d, ROUTE_LANES), lambda i: (0, 0)),
                  pl.BlockSpec((1, ROUTE_LANES), lambda i: (0, 0))],
        out_specs=(pl.BlockSpec((tm, d), lambda i: (i, 0)),
                   pl.BlockSpec((tm, ROUTE_LANES), lambda i: (i, 0))),
        compiler_params=_params("parallel"),
        name="moe_router",
    )(x, g.reshape(1, d), w_route, b_route)


def _expert_kernel(be_ref, nu_ref, x_ref, wg_ref, wu_ref, wd_ref, o_ref, wg_s, wu_s, wd_s):
    b = pl.program_id(0)
    prev = be_ref[jnp.maximum(b - 1, 0)]

    @pl.when((b == 0) | (be_ref[b] != prev))
    def _load_expert():
        wg_s[...] = wg_ref[...].astype(BF16)
        wu_s[...] = wu_ref[...].astype(BF16)
        wd_s[...] = wd_ref[...].astype(BF16)

    @pl.when(b < nu_ref[0])
    def _compute():
        x = x_ref[...].astype(BF16)
        mid = _silu(_dot(x, wg_s[...])) * _dot(x, wu_s[...])
        o_ref[...] = _dot(mid.astype(BF16), wd_s[...])

    @pl.when(b >= nu_ref[0])
    def _unused():
        o_ref[...] = jnp.zeros_like(o_ref)


def moe_experts(xb, blk_e, n_used, w_gate, w_up, w_down, layer, tb=MOE_TILE):
    n_rows, d = xb.shape
    n_blk = n_rows // tb
    w_in_spec = pl.BlockSpec((None, None, d, D_FF), lambda b, be, nu: (layer, be[b], 0, 0))
    grid_spec = pltpu.PrefetchScalarGridSpec(
        num_scalar_prefetch=2,
        grid=(n_blk,),
        in_specs=[pl.BlockSpec((tb, d), lambda b, be, nu: (b, 0)),
                  w_in_spec, w_in_spec,
                  pl.BlockSpec((None, None, D_FF, d), lambda b, be, nu: (layer, be[b], 0, 0))],
        out_specs=pl.BlockSpec((tb, d), lambda b, be, nu: (b, 0)),
        scratch_shapes=[pltpu.VMEM((d, D_FF), BF16), pltpu.VMEM((d, D_FF), BF16), pltpu.VMEM((D_FF, d), BF16)],
    )
    return pl.pallas_call(
        _expert_kernel,
        out_shape=jax.ShapeDtypeStruct((n_rows, d), F32),
        grid_spec=grid_spec,
        compiler_params=_params("arbitrary"),
        name="moe_experts",
    )(blk_e, n_used, xb, w_gate, w_up, w_down)


def _dispatch_plan(experts, tb):
    t, k_sel = experts.shape
    n_asg = t * k_sel
    flat_e = experts.reshape(n_asg)
    onehot = (flat_e[:, None] == jnp.arange(N_EXPERTS, dtype=jnp.int32)[None, :]).astype(jnp.int32)
    csum = jnp.cumsum(onehot, axis=0)
    counts = csum[-1]
    starts = jnp.cumsum(counts) - counts
    padded = (counts + tb - 1) // tb * tb
    pad_end = jnp.cumsum(padded)
    pad_start = pad_end - padded
    dest = jnp.sum(onehot * (csum - 1 + pad_start[None, :]), axis=1)
    n_blk = (n_asg + N_EXPERTS * (tb - 1) + tb - 1) // tb
    blk_row0 = jnp.arange(n_blk, dtype=jnp.int32) * tb
    blk_e = jnp.minimum(jnp.sum((blk_row0[:, None] >= pad_end[None, :]).astype(jnp.int32), axis=1),
                        N_EXPERTS - 1)
    order = jnp.argsort(flat_e).astype(jnp.int32)
    within = (blk_row0 - pad_start[blk_e])[:, None] + jnp.arange(tb, dtype=jnp.int32)[None, :]
    src = jnp.clip(starts[blk_e][:, None] + within, 0, n_asg - 1)
    row_tok = jnp.where(within < counts[blk_e][:, None], order[src.reshape(-1)].reshape(n_blk, tb) // k_sel, 0)
    n_used = (pad_end[-1:] // tb).astype(jnp.int32)
    return row_tok.reshape(-1), dest.reshape(t, k_sel), blk_e.astype(jnp.int32), n_used


def _combine_ple_kernel(x_ref, y0_ref, y1_ref, r_ref, p_ref, g_ref, wg_ref, wp_ref, o_ref):
    route = r_ref[...]
    x = x_ref[...] + route[:, 2:3] * y0_ref[...] + route[:, 3:4] * y1_ref[...]
    h = _rmsnorm(x, g_ref[...]).astype(BF16)
    gate = _sigmoid(_dot(h, wg_ref[...]))
    proj = _dot(p_ref[...].astype(BF16), wp_ref[...])
    o_ref[...] = x + gate * proj


def moe_combine_ple(x, y0, y1, route, p, g, wg_bf16, wp_bf16, tm=512):
    t, d = x.shape
    tm = _row_tile(t, tm)
    pd = p.shape[1]
    row = pl.BlockSpec((tm, d), lambda i: (i, 0))
    return pl.pallas_call(
        _combine_ple_kernel,
        out_shape=jax.ShapeDtypeStruct((t, d), F32),
        grid=(t // tm,),
        in_specs=[row, row, row,
                  pl.BlockSpec((tm, ROUTE_LANES), lambda i: (i, 0)),
                  pl.BlockSpec((tm, pd), lambda i: (i, 0)),
                  pl.BlockSpec((1, d), lambda i: (0, 0)),
                  pl.BlockSpec((d, d), lambda i: (0, 0)),
                  pl.BlockSpec((pd, d), lambda i: (0, 0))],
        out_specs=row,
        compiler_params=_params("parallel"),
        name="moe_combine_ple",
    )(x, y0, y1, route, p, g.reshape(1, d), wg_bf16, wp_bf16)


def moe_ple_add(x, p, layer, norm_g, rg_w, rg_b, re_w, re_b, w_gate, w_up, w_down, ple_g, wg_bf16, wp_bf16,
                tb=MOE_TILE):
    t, d = x.shape
    w_route = jnp.zeros((d, ROUTE_LANES), F32).at[:, :N_GROUPS].set(rg_w).at[:, N_GROUPS:N_GROUPS + N_EXPERTS].set(re_w)
    b_route = jnp.zeros((1, ROUTE_LANES), F32).at[0, :N_GROUPS].set(rg_b).at[0, N_GROUPS:N_GROUPS + N_EXPERTS].set(re_b)
    h, route = moe_router(x, norm_g, w_route, b_route)
    row_tok, dest, blk_e, n_used = _dispatch_plan(route[:, 0:2].astype(jnp.int32), tb)
    yb = moe_experts(h[row_tok], blk_e, n_used, w_gate, w_up, w_down, layer, tb)
    return moe_combine_ple(x, yb[dest[:, 0]], yb[dest[:, 1]], route, p, ple_g, wg_bf16, wp_bf16)


def _lambda_init(layer):
    return 0.8 - 0.6 * math.exp(-0.3 * layer)


def _permute_even_in(w_in):
    d = w_in.shape[0]
    lr0 = 2 * GLA_KEY_W + 2 * GLA_VAL_W
    return jnp.concatenate([w_in[:, :lr0], w_in[:, lr0 + GLA_RANK:], w_in[:, lr0:lr0 + GLA_RANK],
                            jnp.zeros((d, LANES - GLA_RANK), w_in.dtype)], axis=1).astype(BF16)


def kernel(x_prompt, x_sample, cache_k, cache_v, state_gla, state_conv, page_table, p_prompt, p_sample, rel_bias, norm_mix, norm_ffn, norm_ple, w_in_ev, w_lr_up, b_lr, gla_norm, conv_w, conv_b, conv_ln_g, conv_ln_b, w_out_ev, w_qkv, q_norm, k_norm, lam_q1, lam_k1, lam_q2, lam_k2, subln, w_out_od, router_g_w, router_g_b, router_e_w, router_e_b, moe_w_gate, moe_w_up, moe_w_down, ple_proj, ple_gate):
    n_bp, seq, d = x_prompt.shape
    n_bs = x_sample.shape[0]
    depth = norm_mix.shape[0]
    n_layers_odd, n_pool, page = cache_k.shape[:3]
    xp = x_prompt.reshape(n_bp * seq, d)
    xs = x_sample.reshape(n_bs, d)
    ck = jnp.transpose(cache_k, (0, 1, 3, 4, 5, 2)).reshape(n_layers_odd, n_pool, d, page)
    cv = cache_v.reshape(n_layers_odd, n_pool, page * DIFF_HEADS, DIFF_DV)
    prompt_bias = prompt_bias_tables(rel_bias)
    step_bias = step_bias_tables(rel_bias, page_table.shape[1] * page)
    step_rows = 16
    kp_l, vp_l, ks_l, vs_l, gp_l, gs_l, cp_l, cs_l = [], [], [], [], [], [], [], []
    for i in range(depth):
        j = i // 2
        if i % 2 == 0:
            w_in = _permute_even_in(w_in_ev[j])
            w_lr_pad = jnp.zeros((LANES, GLA_KEY_W), BF16).at[:GLA_RANK].set(w_lr_up[j].astype(BF16))
            cw_pad = jnp.zeros((CONV_HALO, CONV_CH), F32).at[:CONV_WIDTH].set(conv_w[j])
            w_out = w_out_ev[j].astype(BF16)
            conv_args = (cw_pad, conv_b[j], conv_ln_g[j], conv_ln_b[j])
            zp = norm_proj(xp, norm_mix[i], w_in)
            op, g_p = gla_mix(zp, n_bp, seq, GLA_CHUNK, 512, w_lr_pad, b_lr[j], gla_norm[j], None)
            cp, c_p = conv_mix(zp, n_bp, seq, *conv_args)
            xp = proj_residual([op, cp], w_out, xp)
            zs = norm_proj(xs, norm_mix[i], w_in)
            zs_pad = jnp.zeros((n_bs, step_rows, EVEN_Z_W), F32).at[:, 0].set(zs).reshape(n_bs * step_rows, EVEN_Z_W)
            os_pad, g_s = gla_mix(zs_pad, n_bs, step_rows, step_rows, step_rows, w_lr_pad, b_lr[j], gla_norm[j],
                                  state_gla[j].reshape(n_bs, GLA_KEY_W, GLA_DV), valid_rows=1)
            os_ = os_pad.reshape(n_bs, step_rows, GLA_VAL_W)[:, 0]
            cs, c_s = conv_step(zs, state_conv[j], *conv_args)
            xs = proj_residual([os_, cs], w_out, xs)
            gp_l.append(g_p.reshape(n_bp, GLA_HEADS, GLA_DK, GLA_DV))
            gs_l.append(g_s.reshape(n_bs, GLA_HEADS, GLA_DK, GLA_DV))
            cp_l.append(c_p)
            cs_l.append(c_s)
        else:
            lam0 = _lambda_init(i)
            lamv = jnp.stack([lam_q1[j], lam_k1[j], lam_q2[j], lam_k2[j]]).astype(F32)
            wq = w_qkv[j].astype(BF16)
            w_out = w_out_od[j].astype(BF16)
            qp, kp, vp, kpb, vpb = qkv_proj(xp, norm_mix[i], wq, q_norm[j], k_norm[j])
            ap = diff_attention(qp, kpb, vpb, n_bp, seq, prompt_bias, lamv, subln[j], lam0)
            xp = proj_residual([ap], w_out, xp)
            qs, ks, vs, _, _ = qkv_proj(xs, norm_mix[i], wq, q_norm[j], k_norm[j])
            as_ = diff_attention_step(qs, ks, vs, ck, cv, j, page_table, step_bias, lamv, subln[j], lam0)
            xs = proj_residual([as_], w_out, xs)
            kp_l.append(kp.reshape(n_bp, seq, DIFF_HEADS, 2, DIFF_DH))
            vp_l.append(vp.reshape(n_bp, seq, DIFF_HEADS, DIFF_DV))
            ks_l.append(ks.reshape(n_bs, 1, DIFF_HEADS, 2, DIFF_DH))
            vs_l.append(vs.reshape(n_bs, 1, DIFF_HEADS, DIFF_DV))
        tail_args = (i, norm_ffn[i], router_g_w[i], router_g_b[i], router_e_w[i], router_e_b[i],
                     moe_w_gate, moe_w_up, moe_w_down, norm_ple[i], ple_gate[i].astype(BF16), ple_proj[i].astype(BF16))
        xp = moe_ple_add(xp, p_prompt[i].reshape(n_bp * seq, PLE_DIM), *tail_args)
        xs = moe_ple_add(xs, p_sample[i].reshape(n_bs, PLE_DIM), *tail_args)
    return (xp.reshape(n_bp, seq, d), xs.reshape(n_bs, 1, d),
            jnp.stack(kp_l), jnp.stack(vp_l), jnp.stack(ks_l), jnp.stack(vs_l),
            jnp.stack(gp_l), jnp.stack(gs_l), jnp.stack(cp_l), jnp.stack(cs_l))
```

```python
import functools
import math

import numpy as np
import jax
import jax.numpy as jnp
from jax import lax
from jax.experimental import pallas as pl
from jax.experimental.pallas import tpu as pltpu

F32 = jnp.float32
BF16 = jnp.bfloat16

D_MODEL = 1024
NORM_EPS = 1e-6
GLA_HEADS = 4
GLA_DK = 64
GLA_DV = 128
GLA_KEY_W = GLA_HEADS * GLA_DK
GLA_VAL_W = GLA_HEADS * GLA_DV
GLA_RANK = 16
GLA_TAU = 16.0
GLA_CHUNK = 64
CONV_CH = 512
CONV_WIDTH = 31
CONV_HALO = 32
DIFF_HEADS = 8
DIFF_DH = 64
DIFF_DV = 128
REL_BUCKETS = 32
REL_MAX_DIST = 128
ATT_TILE = 512
MASK_VALUE = -1e30
N_GROUPS = 4
EXPERTS_PER_GROUP = 8
N_EXPERTS = N_GROUPS * EXPERTS_PER_GROUP
D_FF = 512
MOE_TILE = 256
ROUTE_LANES = 128
PLE_DIM = 256

LANES = 128
VMEM_LIMIT_BYTES = 56 * 1024 * 1024
EVEN_Z_W = 2 * GLA_KEY_W + 2 * GLA_VAL_W + 2 * CONV_CH + LANES
EVEN_LR_BLOCK = (2 * GLA_KEY_W + 2 * GLA_VAL_W + 2 * CONV_CH) // LANES


def _params(*sem):
    return pltpu.CompilerParams(dimension_semantics=sem, vmem_limit_bytes=VMEM_LIMIT_BYTES)


def _dot(a, b):
    return jnp.dot(a, b, preferred_element_type=F32)


def _dot_nt(a, b):
    return lax.dot_general(a, b, (((1,), (1,)), ((), ())), preferred_element_type=F32)


def _dot_tn(a, b):
    return lax.dot_general(a, b, (((0,), (0,)), ((), ())), preferred_element_type=F32)


def _split_bf16(x):
    hi = x.astype(BF16)
    lo = (x - hi.astype(F32)).astype(BF16)
    return hi, lo


def _rmsnorm(x, g):
    ms = jnp.mean(x * x, axis=-1, keepdims=True)
    return x * lax.rsqrt(ms + NORM_EPS) * g


def _sigmoid(x):
    return 1.0 / (1.0 + jnp.exp(-x))


def _silu(x):
    return x * _sigmoid(x)


def _row_tile(n_rows, want):
    t = min(want, n_rows)
    assert n_rows % t == 0, (n_rows, t)
    return t


def _norm_proj_kernel(x_ref, g_ref, w_ref, o_ref):
    h = _rmsnorm(x_ref[...], g_ref[...]).astype(BF16)
    o_ref[...] = _dot(h, w_ref[...])


def norm_proj(x, g, w_bf16, tm=256):
    t, d = x.shape
    n = w_bf16.shape[1]
    tm = _row_tile(t, tm)
    return pl.pallas_call(
        _norm_proj_kernel,
        out_shape=jax.ShapeDtypeStruct((t, n), F32),
        grid=(t // tm,),
        in_specs=[pl.BlockSpec((tm, d), lambda i: (i, 0)),
                  pl.BlockSpec((1, d), lambda i: (0, 0)),
                  pl.BlockSpec((d, n), lambda i: (0, 0))],
        out_specs=pl.BlockSpec((tm, n), lambda i: (i, 0)),
        compiler_params=_params("parallel"),
        name="norm_proj",
    )(x, g.reshape(1, d), w_bf16)


def _proj_residual_kernel(*refs, n_in):
    a_refs = refs[:n_in]
    w_ref, x_ref, o_ref = refs[n_in:]
    acc = x_ref[...]
    row = 0
    for a_ref in a_refs:
        k = a_ref.shape[1]
        acc = acc + _dot(a_ref[...].astype(BF16), w_ref[row:row + k, :])
        row += k
    o_ref[...] = acc


def proj_residual(acts, w_bf16, x, tm=512):
    t, d = x.shape
    tm = _row_tile(t, tm)
    n_in = len(acts)
    in_specs = [pl.BlockSpec((tm, a.shape[1]), lambda i: (i, 0)) for a in acts]
    in_specs += [pl.BlockSpec(w_bf16.shape, lambda i: (0, 0)),
                 pl.BlockSpec((tm, d), lambda i: (i, 0))]
    return pl.pallas_call(
        functools.partial(_proj_residual_kernel, n_in=n_in),
        out_shape=jax.ShapeDtypeStruct((t, d), F32),
        grid=(t // tm,),
        in_specs=in_specs,
        out_specs=pl.BlockSpec((tm, d), lambda i: (i, 0)),
        compiler_params=_params("parallel"),
        name="proj_residual",
    )(*acts, w_bf16, x)


def _ple_kernel(x_ref, p_ref, g_ref, wg_ref, wp_ref, o_ref):
    x = x_ref[...]
    h = _rmsnorm(x, g_ref[...]).astype(BF16)
    gate = _sigmoid(_dot(h, wg_ref[...]))
    proj = _dot(p_ref[...].astype(BF16), wp_ref[...])
    o_ref[...] = x + gate * proj


def ple_add(x, p, g, wg_bf16, wp_bf16, tm=512):
    t, d = x.shape
    tm = _row_tile(t, tm)
    pd = p.shape[1]
    return pl.pallas_call(
        _ple_kernel,
        out_shape=jax.ShapeDtypeStruct((t, d), F32),
        grid=(t // tm,),
        in_specs=[pl.BlockSpec((tm, d), lambda i: (i, 0)),
                  pl.BlockSpec((tm, pd), lambda i: (i, 0)),
                  pl.BlockSpec((1, d), lambda i: (0, 0)),
                  pl.BlockSpec((d, d), lambda i: (0, 0)),
                  pl.BlockSpec((pd, d), lambda i: (0, 0))],
        out_specs=pl.BlockSpec((tm, d), lambda i: (i, 0)),
        compiler_params=_params("parallel"),
        name="ple_add",
    )(x, p, g.reshape(1, d), wg_bf16, wp_bf16)


def _gla_constants(chunk):
    c = chunk
    t = np.arange(c)
    tri = (t[None, :] <= t[:, None]).astype(np.float32)
    causal4 = np.tile(tri, (GLA_HEADS, 1))
    lane_head = np.arange(GLA_KEY_W) // GLA_DK
    row_head = np.repeat(np.arange(GLA_HEADS), c)
    headmask4 = (row_head[:, None] == lane_head[None, :]).astype(np.float32)
    blockdiag = (np.arange(GLA_KEY_W)[:, None] // GLA_DK
                 == np.arange(GLA_VAL_W)[None, :] // GLA_DV).astype(np.float32)
    return (jnp.asarray(tri, BF16), jnp.asarray(causal4), jnp.asarray(headmask4), jnp.asarray(blockdiag))


def _gla_kernel(*refs, chunk, n_chunks, has_state, valid_rows):
    if has_state:
        (qk_ref, v_ref, r_ref, lr_ref, wlr_ref, blr_ref, gg_ref, tri_ref, causal_ref, hmask_ref, bd_ref,
         s0_ref, o_ref, sout_ref, s_scr) = refs
    else:
        (qk_ref, v_ref, r_ref, lr_ref, wlr_ref, blr_ref, gg_ref, tri_ref, causal_ref, hmask_ref, bd_ref,
         o_ref, sout_ref, s_scr) = refs
    c = chunk
    i = pl.program_id(1)
    bd = bd_ref[...]

    @pl.when(i == 0)
    def _init():
        if has_state:
            s0 = s0_ref[0]
            s_scr[...] = jnp.concatenate([s0] * GLA_HEADS, axis=1) * bd
        else:
            s_scr[...] = jnp.zeros_like(s_scr)

    tri = tri_ref[...]
    ones = jnp.ones((c, LANES), BF16)

    def chunk_step(ci, carry):
        rows = pl.ds(pl.multiple_of(ci * c, c), c)
        qk = qk_ref[rows, :]
        q = qk[:, :GLA_KEY_W] * (GLA_DK ** -0.5)
        k = qk[:, GLA_KEY_W:]
        vb = v_ref[rows, :].astype(BF16)
        x = _dot(lr_ref[rows, :].astype(BF16), wlr_ref[...]) + blr_ref[...]
        g = (jnp.minimum(x, 0.0) - jnp.log1p(jnp.exp(-jnp.abs(x)))) * (1.0 / GLA_TAU)
        if valid_rows < c:
            g = jnp.where(lax.broadcasted_iota(jnp.int32, g.shape, 0) < valid_rows, g, 0.0)
        g_hi, g_lo = _split_bf16(g)
        b = _dot(tri, g_hi) + _dot(tri, g_lo)
        dcol = _dot_tn(g_hi, ones) + _dot_tn(g_lo, ones)
        bend = b[c - 1:c, :]
        half = 0.5 * bend
        qs = q * jnp.exp(b - half)
        ks = k * jnp.exp(half - b)
        qt = q * jnp.exp(b)
        kh = k * jnp.exp(bend - b)
        qstack = (jnp.concatenate([qs] * GLA_HEADS, axis=0) * hmask_ref[...]).astype(BF16)
        scores = (_dot_nt(qstack, ks.astype(BF16)) * causal_ref[...]).astype(BF16)
        s_bd = s_scr[...]
        o_inter = _dot(qt.astype(BF16), s_bd.astype(BF16))
        gg = gg_ref[...]
        for h in range(GLA_HEADS):
            cols = slice(h * GLA_DV, (h + 1) * GLA_DV)
            o_h = _dot(scores[h * c:(h + 1) * c, :], vb[:, cols]) + o_inter[:, cols]
            y = _rmsnorm(o_h, gg) * _silu(r_ref[rows, cols])
            o_ref[rows, cols] = y.astype(o_ref.dtype)
        decay = jnp.concatenate([jnp.exp(dcol)] * GLA_HEADS, axis=1)
        s_scr[...] = s_bd * decay + _dot_tn(kh.astype(BF16), vb) * bd
        return carry

    lax.fori_loop(0, n_chunks, chunk_step, 0)

    @pl.when(i == pl.num_programs(1) - 1)
    def _final():
        s_bd = s_scr[...]
        for h in range(GLA_HEADS):
            sout_ref[0, h * GLA_DK:(h + 1) * GLA_DK, :] = s_bd[h * GLA_DK:(h + 1) * GLA_DK,
                                                               h * GLA_DV:(h + 1) * GLA_DV]


def gla_mix(z, n_batch, seq, chunk, tile, w_lr_pad, b_lr, gla_g, state0, valid_rows=None):
    nt = seq // tile
    assert seq % tile == 0 and tile % chunk == 0
    valid_rows = chunk if valid_rows is None else valid_rows
    assert valid_rows == chunk or seq == chunk
    has_state = state0 is not None
    consts = _gla_constants(chunk)
    row = lambda b, i: b * nt + i
    in_specs = [pl.BlockSpec((tile, 2 * GLA_KEY_W), lambda b, i: (row(b, i), 0)),
                pl.BlockSpec((tile, GLA_VAL_W), lambda b, i: (row(b, i), 1)),
                pl.BlockSpec((tile, GLA_VAL_W), lambda b, i: (row(b, i), 2)),
                pl.BlockSpec((tile, LANES), lambda b, i: (row(b, i), EVEN_LR_BLOCK)),
                pl.BlockSpec(w_lr_pad.shape, lambda b, i: (0, 0)),
                pl.BlockSpec((1, GLA_KEY_W), lambda b, i: (0, 0)),
                pl.BlockSpec((1, GLA_DV), lambda b, i: (0, 0))]
    in_specs += [pl.BlockSpec(a.shape, lambda b, i: (0, 0)) for a in consts]
    args = [z, z, z, z, w_lr_pad, b_lr.reshape(1, GLA_KEY_W), gla_g.reshape(1, GLA_DV), *consts]
    if has_state:
        in_specs.append(pl.BlockSpec((1, GLA_KEY_W, GLA_DV), lambda b, i: (b, 0, 0)))
        args.append(state0)
    return pl.pallas_call(
        functools.partial(_gla_kernel, chunk=chunk, n_chunks=tile // chunk, has_state=has_state,
                          valid_rows=valid_rows),
        out_shape=(jax.ShapeDtypeStruct((n_batch * seq, GLA_VAL_W), BF16),
                   jax.ShapeDtypeStruct((n_batch, GLA_KEY_W, GLA_DV), F32)),
        grid=(n_batch, nt),
        in_specs=in_specs,
        out_specs=(pl.BlockSpec((tile, GLA_VAL_W), lambda b, i: (row(b, i), 0)),
                   pl.BlockSpec((1, GLA_KEY_W, GLA_DV), lambda b, i: (b, 0, 0))),
        scratch_shapes=[pltpu.VMEM((GLA_KEY_W, GLA_VAL_W), F32)],
        compiler_params=_params("parallel", "arbitrary"),
        name="gla_mix",
    )(*args)


def _layernorm_silu(c, g, b):
    mu = jnp.mean(c, axis=-1, keepdims=True)
    cc = c - mu
    var = jnp.mean(cc * cc, axis=-1, keepdims=True)
    return _silu(cc * lax.rsqrt(var + NORM_EPS) * g + b)


def _conv_kernel(a_ref, gt_ref, cw_ref, cb_ref, lg_ref, lb_ref, o_ref, st_ref, ext, *, tile, sub):
    i = pl.program_id(1)
    n_state = CONV_WIDTH - 1

    @pl.when(i == 0)
    def _zero_halo():
        ext[0:CONV_HALO, :] = jnp.zeros((CONV_HALO, CONV_CH), F32)

    @pl.when(i > 0)
    def _carry_halo():
        ext[0:CONV_HALO, :] = ext[tile:tile + CONV_HALO, :]

    ext[CONV_HALO:CONV_HALO + tile, :] = a_ref[...] * _sigmoid(gt_ref[...])
    lg = lg_ref[...]
    lb = lb_ref[...]
    for s in range(tile // sub):
        acc = jnp.broadcast_to(cb_ref[...], (sub, CONV_CH))
        for w in range(CONV_WIDTH):
            off = s * sub + (CONV_HALO - n_state) + w
            acc = acc + ext[off:off + sub, :] * cw_ref[w:w + 1, :]
        o_ref[s * sub:(s + 1) * sub, :] = _layernorm_silu(acc, lg, lb).astype(o_ref.dtype)

    @pl.when(i == pl.num_programs(1) - 1)
    def _state():
        st_ref[0] = ext[tile + CONV_HALO - n_state:tile + CONV_HALO, :]


def conv_mix(z, n_batch, seq, conv_w_pad, conv_b, ln_g, ln_b, tile=256, sub=32):
    nt = seq // tile
    assert seq % tile == 0 and tile % sub == 0 and tile >= CONV_HALO
    row = lambda b, i: b * nt + i
    vec = lambda a: a.reshape(1, CONV_CH)
    return pl.pallas_call(
        functools.partial(_conv_kernel, tile=tile, sub=sub),
        out_shape=(jax.ShapeDtypeStruct((n_batch * seq, CONV_CH), BF16),
                   jax.ShapeDtypeStruct((n_batch, CONV_WIDTH - 1, CONV_CH), F32)),
        grid=(n_batch, nt),
        in_specs=[pl.BlockSpec((tile, CONV_CH), lambda b, i: (row(b, i), 3)),
                  pl.BlockSpec((tile, CONV_CH), lambda b, i: (row(b, i), 4)),
                  pl.BlockSpec(conv_w_pad.shape, lambda b, i: (0, 0)),
                  pl.BlockSpec((1, CONV_CH), lambda b, i: (0, 0)),
                  pl.BlockSpec((1, CONV_CH), lambda b, i: (0, 0)),
                  pl.BlockSpec((1, CONV_CH), lambda b, i: (0, 0))],
        out_specs=(pl.BlockSpec((tile, CONV_CH), lambda b, i: (row(b, i), 0)),
                   pl.BlockSpec((1, CONV_WIDTH - 1, CONV_CH), lambda b, i: (b, 0, 0))),
        scratch_shapes=[pltpu.VMEM((tile + CONV_HALO, CONV_CH), F32)],
        compiler_params=_params("parallel", "arbitrary"),
        name="conv_mix",
    )(z, z, conv_w_pad, vec(conv_b), vec(ln_g), vec(ln_b))


def _conv_step_kernel(a_ref, gt_ref, st_ref, cw_ref, cb_ref, lg_ref, lb_ref, o_ref, sto_ref):
    n_state = CONV_WIDTH - 1
    u = a_ref[...] * _sigmoid(gt_ref[...])
    acc = cb_ref[...] + u * cw_ref[n_state:n_state + 1, :]
    for w in range(n_state):
        acc = acc + st_ref[:, w * CONV_CH:(w + 1) * CONV_CH] * cw_ref[w:w + 1, :]
    o_ref[...] = _layernorm_silu(acc, lg_ref[...], lb_ref[...]).astype(o_ref.dtype)
    sto_ref[:, 0:(n_state - 1) * CONV_CH] = st_ref[:, CONV_CH:n_state * CONV_CH]
    sto_ref[:, (n_state - 1) * CONV_CH:] = u


def conv_step(z, state, conv_w_pad, conv_b, ln_g, ln_b, tile=32):
    nb = z.shape[0]
    n_state = CONV_WIDTH - 1
    tile = _row_tile(nb, tile)
    vec = lambda a: a.reshape(1, CONV_CH)
    st2 = state.reshape(nb, n_state * CONV_CH)
    out, st_new = pl.pallas_call(
        _conv_step_kernel,
        out_shape=(jax.ShapeDtypeStruct((nb, CONV_CH), BF16),
                   jax.ShapeDtypeStruct((nb, n_state * CONV_CH), F32)),
        grid=(nb // tile,),
        in_specs=[pl.BlockSpec((tile, CONV_CH), lambda i: (i, 3)),
                  pl.BlockSpec((tile, CONV_CH), lambda i: (i, 4)),
                  pl.BlockSpec((tile, n_state * CONV_CH), lambda i: (i, 0)),
                  pl.BlockSpec(conv_w_pad.shape, lambda i: (0, 0)),
                  pl.BlockSpec((1, CONV_CH), lambda i: (0, 0)),
                  pl.BlockSpec((1, CONV_CH), lambda i: (0, 0)),
                  pl.BlockSpec((1, CONV_CH), lambda i: (0, 0))],
        out_specs=(pl.BlockSpec((tile, CONV_CH), lambda i: (i, 0)),
                   pl.BlockSpec((tile, n_state * CONV_CH), lambda i: (i, 0))),
        compiler_params=_params("parallel"),
        name="conv_step",
    )(z, z, st2, conv_w_pad, vec(conv_b), vec(ln_g), vec(ln_b))
    return out, st_new.reshape(nb, n_state, CONV_CH)


def _group_constants():
    grp = np.zeros((D_MODEL, LANES), np.float32)
    grp[np.arange(D_MODEL), np.arange(D_MODEL) // DIFF_DH] = 1.0
    return jnp.asarray(grp, BF16), jnp.asarray(grp.T.copy(), BF16)


def _qkv_kernel(x_ref, g_ref, w_ref, grp_ref, grpt_ref, qg_ref, kg_ref, q_ref, k_ref, v_ref, kb_ref, vb_ref):
    h = _rmsnorm(x_ref[...], g_ref[...]).astype(BF16)
    z = _dot(h, w_ref[...])
    grp = grp_ref[...]
    grpt = grpt_ref[...]

    def map_norm(y, gain):
        hi, lo = _split_bf16(y * y)
        ms = (_dot(hi, grp) + _dot(lo, grp)) * (1.0 / DIFF_DH)
        rh, rl = _split_bf16(lax.rsqrt(ms + NORM_EPS))
        return y * (_dot(rh, grpt) + _dot(rl, grpt)) * gain

    qn = map_norm(z[:, :D_MODEL], qg_ref[...])
    kn = map_norm(z[:, D_MODEL:2 * D_MODEL], kg_ref[...])
    v = z[:, 2 * D_MODEL:]
    q_ref[...] = (qn * (DIFF_DH ** -0.5)).astype(BF16)
    k_ref[...] = kn
    v_ref[...] = v
    kb_ref[...] = kn.astype(BF16)
    ones = jnp.ones((v.shape[0], DIFF_DV), BF16)
    for h in range(DIFF_HEADS):
        vb_ref[:, 2 * h * DIFF_DV:(2 * h + 1) * DIFF_DV] = v[:, h * DIFF_DV:(h + 1) * DIFF_DV].astype(BF16)
        vb_ref[:, (2 * h + 1) * DIFF_DV:(2 * h + 2) * DIFF_DV] = ones


def qkv_proj(x, g, w_bf16, q_gain, k_gain, tm=256):
    t, d = x.shape
    tm = _row_tile(t, tm)
    grp, grpt = _group_constants()
    tile_gain = lambda a: jnp.tile(a, D_MODEL // DIFF_DH).reshape(1, D_MODEL)
    row = pl.BlockSpec((tm, d), lambda i: (i, 0))
    full = lambda a: pl.BlockSpec(a.shape, lambda i: (0, 0))
    qg, kg = tile_gain(q_gain), tile_gain(k_gain)
    return pl.pallas_call(
        _qkv_kernel,
        out_shape=(jax.ShapeDtypeStruct((t, d), BF16), jax.ShapeDtypeStruct((t, d), F32),
                   jax.ShapeDtypeStruct((t, d), F32), jax.ShapeDtypeStruct((t, d), BF16),
                   jax.ShapeDtypeStruct((t, 2 * d), BF16)),
        grid=(t // tm,),
        in_specs=[row, pl.BlockSpec((1, d), lambda i: (0, 0)), full(w_bf16), full(grp), full(grpt),
                  full(qg), full(kg)],
        out_specs=(row, row, row, row, pl.BlockSpec((tm, 2 * d), lambda i: (i, 0))),
        compiler_params=_params("parallel"),
        name="qkv_proj",
    )(x, g.reshape(1, d), w_bf16, grp, grpt, qg, kg)


def _rel_bucket_np(n):
    max_exact = REL_BUCKETS // 2
    nf = np.maximum(n, 1).astype(np.float64)
    large = max_exact + (np.log(nf / max_exact) / math.log(REL_MAX_DIST / max_exact)
                         * (REL_BUCKETS - max_exact)).astype(np.int32)
    return np.where(n < max_exact, np.maximum(n, 0), np.minimum(large, REL_BUCKETS - 1))


def _diff_lambda(lamv, lam0):
    s1 = jnp.sum(lamv[0:1, :] * lamv[1:2, :], axis=-1, keepdims=True)
    s2 = jnp.sum(lamv[2:3, :] * lamv[3:4, :], axis=-1, keepdims=True)
    return jnp.exp(s1) - jnp.exp(s2) + lam0


def _attn_kernel(q_ref, k_ref, v_ref, bdiag_ref, bsub_ref, bfar_ref, lamv_ref, sub_ref, o_ref,
                 m_scr, acc_scr, *, tile, lam0):
    qi = pl.program_id(2)
    q = q_ref[...]
    lane = lax.broadcasted_iota(jnp.int32, q.shape, 1)
    zero = jnp.zeros_like(q)
    q_maps = (jnp.where(lane < DIFF_DH, q, zero), jnp.where(lane >= DIFF_DH, q, zero))
    reps = tile // LANES

    m_scr[...] = jnp.full(m_scr.shape, -jnp.inf, F32)
    acc_scr[...] = jnp.zeros(acc_scr.shape, F32)

    def flash_step(kt, bias):
        rows = pl.ds(pl.multiple_of(kt * tile, tile), tile)
        k = k_ref[rows, :]
        v = v_ref[rows, :]
        for m in range(2):
            s = _dot_nt(q_maps[m], k) + bias
            m_prev = m_scr[m]
            m_new = jnp.maximum(m_prev, jnp.max(s, axis=1, keepdims=True))
            alpha = jnp.exp(m_prev - m_new)
            p = jnp.exp(s - jnp.concatenate([m_new] * reps, axis=1))
            acc_scr[m] = jnp.concatenate([alpha, alpha], axis=1) * acc_scr[m] + _dot(p.astype(BF16), v)
            m_scr[m] = m_new

    n_far = jnp.maximum(qi - 1, 0)

    def far_pair(i, carry):
        flash_step(2 * i, bfar_ref[0])
        flash_step(2 * i + 1, bfar_ref[0])
        return carry

    lax.fori_loop(0, n_far // 2, far_pair, 0)

    @pl.when(lax.rem(n_far, 2) == 1)
    def _far_tail():
        flash_step(n_far - 1, bfar_ref[0])

    @pl.when(qi >= 1)
    def _sub_and_diag():
        flash_step(qi - 1, bsub_ref[0])
        flash_step(qi, bdiag_ref[0])

    @pl.when(qi == 0)
    def _diag_only():
        flash_step(qi, bdiag_ref[0])

    lam = _diff_lambda(lamv_ref[...], lam0)
    a0 = acc_scr[0]
    a1 = acc_scr[1]
    o = a0[:, :DIFF_DV] / a0[:, DIFF_DV:] - lam * (a1[:, :DIFF_DV] / a1[:, DIFF_DV:])
    o_ref[...] = (_rmsnorm(o, sub_ref[...]) * (1.0 - lam0)).astype(o_ref.dtype)


def _bucket_lookup(rel_bias, n):
    onehot = np.zeros((n.size, REL_BUCKETS), np.float32)
    onehot[np.arange(n.size), _rel_bucket_np(n).reshape(-1)] = 1.0
    out = jnp.dot(jnp.asarray(onehot), rel_bias.astype(F32), precision=lax.Precision.HIGHEST)
    return out.reshape(n.shape + (rel_bias.shape[1],))


def prompt_bias_tables(rel_bias, tile=ATT_TILE):
    r = np.arange(tile)
    n_diag = r[:, None] - r[None, :]
    bdiag = jnp.where(jnp.asarray(n_diag >= 0)[None], jnp.moveaxis(_bucket_lookup(rel_bias, n_diag), -1, 0),
                      MASK_VALUE)
    bsub = jnp.moveaxis(_bucket_lookup(rel_bias, n_diag + tile), -1, 0)
    assert tile + 1 >= REL_MAX_DIST
    bfar = jnp.broadcast_to(rel_bias[REL_BUCKETS - 1][:, None, None], (DIFF_HEADS, 1, tile))
    return bdiag.astype(F32), bsub.astype(F32), bfar.astype(F32)


def diff_attention(q, kb, vb1, n_batch, seq, bias_tables, lamv, subln, lam0, tile=ATT_TILE):
    nq = seq // tile
    assert seq % tile == 0
    bdiag, bsub, bfar = bias_tables
    head_tile = lambda b, h, i: (h, 0, 0)
    return pl.pallas_call(
        functools.partial(_attn_kernel, tile=tile, lam0=lam0),
        out_shape=jax.ShapeDtypeStruct((n_batch * seq, D_MODEL), BF16),
        grid=(n_batch, DIFF_HEADS, nq),
        in_specs=[pl.BlockSpec((tile, DIFF_DV), lambda b, h, i: (b * nq + i, h)),
                  pl.BlockSpec((seq, DIFF_DV), lambda b, h, i: (b, h)),
                  pl.BlockSpec((seq, 2 * DIFF_DV), lambda b, h, i: (b, h)),
                  pl.BlockSpec((1, tile, tile), head_tile),
                  pl.BlockSpec((1, tile, tile), head_tile),
                  pl.BlockSpec((1, 1, tile), head_tile),
                  pl.BlockSpec((4, DIFF_DH), lambda b, h, i: (0, 0)),
                  pl.BlockSpec((1, DIFF_DV), lambda b, h, i: (0, 0))],
        out_specs=pl.BlockSpec((tile, DIFF_DV), lambda b, h, i: (b * nq + i, h)),
        scratch_shapes=[pltpu.VMEM((2, tile, LANES), F32), pltpu.VMEM((2, tile, 2 * DIFF_DV), F32)],
        compiler_params=_params("parallel", "parallel", "arbitrary"),
        name="diff_attention",
    )(q, kb, vb1, bdiag, bsub, bfar, lamv, subln.reshape(1, DIFF_DV))


N_MAPS = 2 * DIFF_HEADS


def step_bias_tables(rel_bias, n_past):
    n = n_past - np.arange(n_past)
    past = _bucket_lookup(rel_bias, n).T
    new = jnp.broadcast_to(_bucket_lookup(rel_bias, np.zeros((1,), np.int64)).T, (DIFF_HEADS, LANES))
    return jnp.concatenate([past, past], axis=0), jnp.concatenate([new, new], axis=0)


def _step_constants(page):
    j = np.arange(N_MAPS)
    c = np.arange(D_MODEL)
    col_head, col_map = c // DIFF_DV, (c % DIFF_DV) // DIFF_DH
    sel = ((j[:, None] % DIFF_HEADS == col_head[None, :])
           & (j[:, None] // DIFF_HEADS == col_map[None, :])).astype(np.float32)
    r = np.arange(page * DIFF_HEADS)
    tok_of_row = (r[:, None] // DIFF_HEADS == np.arange(page)[None, :]).astype(np.float32)
    return jnp.asarray(sel), jnp.asarray(tok_of_row)


def _attn_step_kernel(pt_ref, q_ref, kn_ref, vn_ref, *refs, n_pages, page, lam0):
    kt_refs = refs[:n_pages]
    v_refs = refs[n_pages:2 * n_pages]
    (bias_ref, nbias_ref, sel_ref, tok_ref, lamv_ref, sub_ref, o_ref, s_scr) = refs[2 * n_pages:]
    sel = sel_ref[...]
    q_sel = sel * q_ref[0].astype(F32)
    q_bf = q_sel.astype(BF16)
    for p in range(n_pages):
        cols = slice(p * page, (p + 1) * page)
        s_scr[:, cols] = _dot(q_bf, kt_refs[p][...].astype(BF16)) + bias_ref[:, cols]
    s_new = jnp.sum(q_sel * kn_ref[0], axis=1, keepdims=True) + nbias_ref[:, 0:1]
    s = s_scr[...]
    m = jnp.maximum(jnp.max(s, axis=1, keepdims=True), s_new)
    p_all = jnp.exp(s - m)
    p_new = jnp.exp(s_new - m)
    l = jnp.sum(p_all, axis=1, keepdims=True) + p_new
    lam = _diff_lambda(lamv_ref[...], lam0)
    row = lax.broadcasted_iota(jnp.int32, (N_MAPS, 1), 0)
    coef = jnp.where(row < DIFF_HEADS, 1.0 / l, -lam / l)
    w_all = p_all * coef
    w_heads = w_all[:DIFF_HEADS, :] + w_all[DIFF_HEADS:, :]
    w_new = (p_new * coef)[:DIFF_HEADS, :] + (p_new * coef)[DIFF_HEADS:, :]
    tok_of_row = tok_ref[...]
    acc = w_new * vn_ref[0]
    for p in range(n_pages):
        w_p = w_heads[:, p * page:(p + 1) * page]
        w_rows = jnp.broadcast_to(w_p[None], (page, DIFF_HEADS, page)).reshape(page * DIFF_HEADS, page)
        w_col = jnp.sum(w_rows * tok_of_row, axis=1, keepdims=True)
        acc = acc + jnp.sum((w_col * v_refs[p][...]).reshape(page, DIFF_HEADS, DIFF_DV), axis=0)
    o_ref[0] = _rmsnorm(acc, sub_ref[...]) * (1.0 - lam0)


def diff_attention_step(q, k_new, v_new, cache_kt, cache_v2, layer, page_table, bias_tables, lamv, subln, lam0):
    nb = q.shape[0]
    n_pages = page_table.shape[1]
    page = cache_kt.shape[3]
    n_past = n_pages * page
    past_bias, new_bias = bias_tables
    sel, tok_of_row = _step_constants(page)
    tok_spec = pl.BlockSpec((1, 1, D_MODEL), lambda b, pt: (b, 0, 0))
    head_spec = pl.BlockSpec((1, DIFF_HEADS, DIFF_DV), lambda b, pt: (b, 0, 0))
    kt_spec = lambda p: pl.BlockSpec((None, None, D_MODEL, page),
                                     lambda b, pt: (layer, pt[b * n_pages + p], 0, 0))
    v_spec = lambda p: pl.BlockSpec((None, None, page * DIFF_HEADS, DIFF_DV),
                                    lambda b, pt: (layer, pt[b * n_pages + p], 0, 0))
    full = lambda a: pl.BlockSpec(a.shape, lambda b, pt: (0,) * a.ndim)
    sub2 = subln.reshape(1, DIFF_DV)
    grid_spec = pltpu.PrefetchScalarGridSpec(
        num_scalar_prefetch=1,
        grid=(nb,),
        in_specs=([tok_spec, tok_spec, head_spec]
                  + [kt_spec(p) for p in range(n_pages)] + [v_spec(p) for p in range(n_pages)]
                  + [full(past_bias), full(new_bias), full(sel), full(tok_of_row), full(lamv), full(sub2)]),
        out_specs=head_spec,
        scratch_shapes=[pltpu.VMEM((N_MAPS, n_past), F32)],
    )
    out = pl.pallas_call(
        functools.partial(_attn_step_kernel, n_pages=n_pages, page=page, lam0=lam0),
        out_shape=jax.ShapeDtypeStruct((nb, DIFF_HEADS, DIFF_DV), F32),
        grid_spec=grid_spec,
        compiler_params=_params("parallel"),
        name="diff_attention_step",
    )(page_table.reshape(-1), q.reshape(nb, 1, D_MODEL), k_new.reshape(nb, 1, D_MODEL),
      v_new.reshape(nb, DIFF_HEADS, DIFF_DV),
      *([cache_kt] * n_pages), *([cache_v2] * n_pages), past_bias, new_bias, sel, tok_of_row, lamv, sub2)
    return out.reshape(nb, D_MODEL)


def _router_kernel(x_ref, g_ref, w_ref, b_ref, h_ref, r_ref):
    h = _rmsnorm(x_ref[...], g_ref[...])
    h_ref[...] = h
    logits = jnp.dot(h, w_ref[...], precision=lax.Precision.HIGHEST, preferred_element_type=F32) + b_ref[...]
    lane = lax.broadcasted_iota(jnp.int32, logits.shape, 1)
    neg = jnp.float32(-jnp.inf)
    big = jnp.int32(2 ** 30)

    def first_argmax(vals):
        mx = jnp.max(vals, axis=1, keepdims=True)
        idx = jnp.min(jnp.where(vals == mx, lane, big), axis=1, keepdims=True)
        return mx, idx

    lg = jnp.where(lane < N_GROUPS, logits, neg)
    g_max, grp = first_argmax(lg)
    w_grp = 1.0 / jnp.sum(jnp.exp(lg - g_max), axis=1, keepdims=True)
    e_lane = lane - N_GROUPS
    in_grp = (e_lane >= grp * EXPERTS_PER_GROUP) & (e_lane < (grp + 1) * EXPERTS_PER_GROUP)
    le = jnp.where(in_grp, logits, neg)
    m1, i1 = first_argmax(le)
    m2, i2 = first_argmax(jnp.where(lane == i1, neg, le))
    e2 = jnp.exp(m2 - m1)
    gate1 = w_grp / (1.0 + e2)
    gate2 = w_grp * e2 / (1.0 + e2)
    f = lambda a: a.astype(F32)
    r_ref[...] = jnp.where(lane == 0, f(i1 - N_GROUPS),
                           jnp.where(lane == 1, f(i2 - N_GROUPS),
                                     jnp.where(lane == 2, gate1, jnp.where(lane == 3, gate2, 0.0))))


def moe_router(x, g, w_route, b_route, tm=512):
    t, d = x.shape
    tm = _row_tile(t, tm)
    return pl.pallas_call(
        _router_kernel,
        out_shape=(jax.ShapeDtypeStruct((t, d), F32), jax.ShapeDtypeStruct((t, ROUTE_LANES), F32)),
        grid=(t // tm,),
        in_specs=[pl.BlockSpec((tm, d), lambda i: (i, 0)),
                  pl.BlockSpec((1, d), lambda i: (0, 0)),
                  pl.BlockSpec((d, ROUTE_LANES), lambda i: (0, 0)),
                  pl.BlockSpec((1, ROUTE_LANES), lambda i: (0, 0))],
        out_specs=(pl.BlockSpec((tm, d), lambda i: (i, 0)),
                   pl.BlockSpec((tm, ROUTE_LANES), lambda i: (i, 0))),
        compiler_params=_params("parallel"),
        name="moe_router",
    )(x, g.reshape(1, d), w_route, b_route)


def _expert_kernel(be_ref, nu_ref, tok_cur_ref, tok_next_ref, h_hbm, wg_ref, wu_ref, wd_ref, o_ref,
                   x_buf, x_sem, wg_s, wu_s, wd_s, *, tb):
    b = pl.program_id(0)
    n_blk = pl.num_programs(0)
    slot = lax.rem(b, 2)

    def start_row(tok_ref, dst_slot, r):
        pltpu.make_async_copy(h_hbm.at[pl.ds(tok_ref[0, r], 1), :], x_buf.at[dst_slot, pl.ds(r, 1), :],
                              x_sem.at[dst_slot]).start()

    def start_gather(tok_ref, dst_slot, inline):
        if inline:
            for r in range(tb):
                start_row(tok_ref, dst_slot, r)
        else:
            def row(r, carry):
                start_row(tok_ref, dst_slot, r)
                return carry
            lax.fori_loop(0, tb, row, 0, unroll=8)

    def wait_gather(dst_slot):
        pltpu.make_async_copy(h_hbm.at[pl.ds(0, tb), :], x_buf.at[dst_slot], x_sem.at[dst_slot]).wait()

    @pl.when(b == 0)
    def _first():
        start_gather(tok_cur_ref, 0, inline=False)

    wait_gather(slot)
    prev = be_ref[jnp.maximum(b - 1, 0)]

    @pl.when((b == 0) | (be_ref[b] != prev))
    def _load_expert():
        wg_s[...] = wg_ref[...].astype(BF16)
        wu_s[...] = wu_ref[...].astype(BF16)
        wd_s[...] = wd_ref[...].astype(BF16)

    @pl.when(b < nu_ref[0])
    def _compute():
        start_gather(tok_next_ref, 1 - slot, inline=True)
        x = x_buf[slot].astype(BF16)
        mid = _silu(_dot(x, wg_s[...])) * _dot(x, wu_s[...])
        o_ref[...] = _dot(mid.astype(BF16), wd_s[...])

    @pl.when(b >= nu_ref[0])
    def _unused():
        start_gather(tok_next_ref, 1 - slot, inline=False)
        o_ref[...] = jnp.zeros_like(o_ref)

    @pl.when(b == n_blk - 1)
    def _drain():
        wait_gather(1 - slot)


def moe_experts(h, row_tok, blk_e, n_used, w_gate, w_up, w_down, layer, tb=MOE_TILE):
    d = h.shape[1]
    n_blk = row_tok.shape[0] // tb
    tok3 = row_tok.reshape(n_blk, 1, tb)
    w_in_spec = pl.BlockSpec((None, None, d, D_FF), lambda b, be, nu: (layer, be[b], 0, 0))
    tok_spec = lambda nxt: pl.BlockSpec((None, 1, tb), lambda b, be, nu: (jnp.minimum(b + nxt, n_blk - 1), 0, 0),
                                        memory_space=pltpu.SMEM)
    grid_spec = pltpu.PrefetchScalarGridSpec(
        num_scalar_prefetch=2,
        grid=(n_blk,),
        in_specs=[tok_spec(0), tok_spec(1),
                  pl.BlockSpec(memory_space=pl.ANY),
                  w_in_spec, w_in_spec,
                  pl.BlockSpec((None, None, D_FF, d), lambda b, be, nu: (layer, be[b], 0, 0))],
        out_specs=pl.BlockSpec((tb, d), lambda b, be, nu: (b, 0)),
        scratch_shapes=[pltpu.VMEM((2, tb, d), F32), pltpu.SemaphoreType.DMA((2,)),
                        pltpu.VMEM((d, D_FF), BF16), pltpu.VMEM((d, D_FF), BF16), pltpu.VMEM((D_FF, d), BF16)],
    )
    return pl.pallas_call(
        functools.partial(_expert_kernel, tb=tb),
        out_shape=jax.ShapeDtypeStruct((n_blk * tb, d), F32),
        grid_spec=grid_spec,
        compiler_params=_params("arbitrary"),
        name="moe_experts",
    )(blk_e, n_used, tok3, tok3, h, w_gate, w_up, w_down)


def _dispatch_plan(experts, tb):
    t, k_sel = experts.shape
    n_asg = t * k_sel
    flat_e = experts.reshape(n_asg)
    onehot = (flat_e[:, None] == jnp.arange(N_EXPERTS, dtype=jnp.int32)[None, :]).astype(jnp.int32)
    csum = jnp.cumsum(onehot, axis=0)
    counts = csum[-1]
    starts = jnp.cumsum(counts) - counts
    padded = (counts + tb - 1) // tb * tb
    pad_end = jnp.cumsum(padded)
    pad_start = pad_end - padded
    dest = jnp.sum(onehot * (csum - 1 + pad_start[None, :]), axis=1)
    n_blk = (n_asg + N_EXPERTS * (tb - 1) + tb - 1) // tb
    blk_row0 = jnp.arange(n_blk, dtype=jnp.int32) * tb
    blk_e = jnp.minimum(jnp.sum((blk_row0[:, None] >= pad_end[None, :]).astype(jnp.int32), axis=1),
                        N_EXPERTS - 1)
    order = jnp.argsort(flat_e).astype(jnp.int32)
    within = (blk_row0 - pad_start[blk_e])[:, None] + jnp.arange(tb, dtype=jnp.int32)[None, :]
    src = jnp.clip(starts[blk_e][:, None] + within, 0, n_asg - 1)
    row_tok = jnp.where(within < counts[blk_e][:, None], order[src.reshape(-1)].reshape(n_blk, tb) // k_sel, 0)
    n_used = (pad_end[-1:] // tb).astype(jnp.int32)
    return row_tok.reshape(-1), dest.reshape(t, k_sel), blk_e.astype(jnp.int32), n_used


def _combine_ple_kernel(x_ref, y0_ref, y1_ref, r_ref, p_ref, g_ref, wg_ref, wp_ref, o_ref):
    route = r_ref[...]
    x = x_ref[...] + route[:, 2:3] * y0_ref[...] + route[:, 3:4] * y1_ref[...]
    h = _rmsnorm(x, g_ref[...]).astype(BF16)
    gate = _sigmoid(_dot(h, wg_ref[...]))
    proj = _dot(p_ref[...].astype(BF16), wp_ref[...])
    o_ref[...] = x + gate * proj


def moe_combine_ple(x, y0, y1, route, p, g, wg_bf16, wp_bf16, tm=512):
    t, d = x.shape
    tm = _row_tile(t, tm)
    pd = p.shape[1]
    row = pl.BlockSpec((tm, d), lambda i: (i, 0))
    return pl.pallas_call(
        _combine_ple_kernel,
        out_shape=jax.ShapeDtypeStruct((t, d), F32),
        grid=(t // tm,),
        in_specs=[row, row, row,
                  pl.BlockSpec((tm, ROUTE_LANES), lambda i: (i, 0)),
                  pl.BlockSpec((tm, pd), lambda i: (i, 0)),
                  pl.BlockSpec((1, d), lambda i: (0, 0)),
                  pl.BlockSpec((d, d), lambda i: (0, 0)),
                  pl.BlockSpec((pd, d), lambda i: (0, 0))],
        out_specs=row,
        compiler_params=_params("parallel"),
        name="moe_combine_ple",
    )(x, y0, y1, route, p, g.reshape(1, d), wg_bf16, wp_bf16)


def moe_ple_add(x, p, layer, norm_g, rg_w, rg_b, re_w, re_b, w_gate, w_up, w_down, ple_g, wg_bf16, wp_bf16,
                tb=MOE_TILE):
    t, d = x.shape
    w_route = jnp.zeros((d, ROUTE_LANES), F32).at[:, :N_GROUPS].set(rg_w).at[:, N_GROUPS:N_GROUPS + N_EXPERTS].set(re_w)
    b_route = jnp.zeros((1, ROUTE_LANES), F32).at[0, :N_GROUPS].set(rg_b).at[0, N_GROUPS:N_GROUPS + N_EXPERTS].set(re_b)
    h, route = moe_router(x, norm_g, w_route, b_route)
    row_tok, dest, blk_e, n_used = _dispatch_plan(route[:, 0:2].astype(jnp.int32), tb)
    yb = moe_experts(h, row_tok, blk_e, n_used, w_gate, w_up, w_down, layer, tb)
    return moe_combine_ple(x, yb[dest[:, 0]], yb[dest[:, 1]], route, p, ple_g, wg_bf16, wp_bf16)


def _lambda_init(layer):
    return 0.8 - 0.6 * math.exp(-0.3 * layer)


def _permute_even_in(w_in):
    d = w_in.shape[0]
    lr0 = 2 * GLA_KEY_W + 2 * GLA_VAL_W
    return jnp.concatenate([w_in[:, :lr0], w_in[:, lr0 + GLA_RANK:], w_in[:, lr0:lr0 + GLA_RANK],
                            jnp.zeros((d, LANES - GLA_RANK), w_in.dtype)], axis=1).astype(BF16)


def kernel(x_prompt, x_sample, cache_k, cache_v, state_gla, state_conv, page_table, p_prompt, p_sample, rel_bias, norm_mix, norm_ffn, norm_ple, w_in_ev, w_lr_up, b_lr, gla_norm, conv_w, conv_b, conv_ln_g, conv_ln_b, w_out_ev, w_qkv, q_norm, k_norm, lam_q1, lam_k1, lam_q2, lam_k2, subln, w_out_od, router_g_w, router_g_b, router_e_w, router_e_b, moe_w_gate, moe_w_up, moe_w_down, ple_proj, ple_gate):
    n_bp, seq, d = x_prompt.shape
    n_bs = x_sample.shape[0]
    depth = norm_mix.shape[0]
    n_layers_odd, n_pool, page = cache_k.shape[:3]
    xp = x_prompt.reshape(n_bp * seq, d)
    xs = x_sample.reshape(n_bs, d)
    ck = jnp.transpose(cache_k, (0, 1, 3, 4, 5, 2)).reshape(n_layers_odd, n_pool, d, page)
    cv = cache_v.reshape(n_layers_odd, n_pool, page * DIFF_HEADS, DIFF_DV)
    prompt_bias = prompt_bias_tables(rel_bias)
    step_bias = step_bias_tables(rel_bias, page_table.shape[1] * page)
    step_rows = 16
    kp_l, vp_l, ks_l, vs_l, gp_l, gs_l, cp_l, cs_l = [], [], [], [], [], [], [], []
    for i in range(depth):
        j = i // 2
        if i % 2 == 0:
            w_in = _permute_even_in(w_in_ev[j])
            w_lr_pad = jnp.zeros((LANES, GLA_KEY_W), BF16).at[:GLA_RANK].set(w_lr_up[j].astype(BF16))
            cw_pad = jnp.zeros((CONV_HALO, CONV_CH), F32).at[:CONV_WIDTH].set(conv_w[j])
            w_out = w_out_ev[j].astype(BF16)
            conv_args = (cw_pad, conv_b[j], conv_ln_g[j], conv_ln_b[j])
            zp = norm_proj(xp, norm_mix[i], w_in)
            op, g_p = gla_mix(zp, n_bp, seq, GLA_CHUNK, 512, w_lr_pad, b_lr[j], gla_norm[j], None)
            cp, c_p = conv_mix(zp, n_bp, seq, *conv_args)
            xp = proj_residual([op, cp], w_out, xp)
            zs = norm_proj(xs, norm_mix[i], w_in)
            zs_pad = jnp.zeros((n_bs, step_rows, EVEN_Z_W), F32).at[:, 0].set(zs).reshape(n_bs * step_rows, EVEN_Z_W)
            os_pad, g_s = gla_mix(zs_pad, n_bs, step_rows, step_rows, step_rows, w_lr_pad, b_lr[j], gla_norm[j],
                                  state_gla[j].reshape(n_bs, GLA_KEY_W, GLA_DV), valid_rows=1)
            os_ = os_pad.reshape(n_bs, step_rows, GLA_VAL_W)[:, 0]
            cs, c_s = conv_step(zs, state_conv[j], *conv_args)
            xs = proj_residual([os_, cs], w_out, xs)
            gp_l.append(g_p.reshape(n_bp, GLA_HEADS, GLA_DK, GLA_DV))
            gs_l.append(g_s.reshape(n_bs, GLA_HEADS, GLA_DK, GLA_DV))
            cp_l.append(c_p)
            cs_l.append(c_s)
        else:
            lam0 = _lambda_init(i)
            lamv = jnp.stack([lam_q1[j], lam_k1[j], lam_q2[j], lam_k2[j]]).astype(F32)
            wq = w_qkv[j].astype(BF16)
            w_out = w_out_od[j].astype(BF16)
            qp, kp, vp, kpb, vpb = qkv_proj(xp, norm_mix[i], wq, q_norm[j], k_norm[j])
            ap = diff_attention(qp, kpb, vpb, n_bp, seq, prompt_bias, lamv, subln[j], lam0)
            xp = proj_residual([ap], w_out, xp)
            qs, ks, vs, _, _ = qkv_proj(xs, norm_mix[i], wq, q_norm[j], k_norm[j])
            as_ = diff_attention_step(qs, ks, vs, ck, cv, j, page_table, step_bias, lamv, subln[j], lam0)
            xs = proj_residual([as_], w_out, xs)
            kp_l.append(kp.reshape(n_bp, seq, DIFF_HEADS, 2, DIFF_DH))
            vp_l.append(vp.reshape(n_bp, seq, DIFF_HEADS, DIFF_DV))
            ks_l.append(ks.reshape(n_bs, 1, DIFF_HEADS, 2, DIFF_DH))
            vs_l.append(vs.reshape(n_bs, 1, DIFF_HEADS, DIFF_DV))
        tail_args = (i, norm_ffn[i], router_g_w[i], router_g_b[i], router_e_w[i], router_e_b[i],
                     moe_w_gate, moe_w_up, moe_w_down, norm_ple[i], ple_gate[i].astype(BF16), ple_proj[i].astype(BF16))
        xp = moe_ple_add(xp, p_prompt[i].reshape(n_bp * seq, PLE_DIM), *tail_args)
        xs = moe_ple_add(xs, p_sample[i].reshape(n_bs, PLE_DIM), *tail_args)
    return (xp.reshape(n_bp, seq, d), xs.reshape(n_bs, 1, d),
            jnp.stack(kp_l), jnp.stack(vp_l), jnp.stack(ks_l), jnp.stack(vs_l),
            jnp.stack(gp_l), jnp.stack(gs_l), jnp.stack(cp_l), jnp.stack(cs_l))
```

```python
import functools
import math

import numpy as np
import jax
import jax.numpy as jnp
from jax import lax
from jax.experimental import pallas as pl
from jax.experimental.pallas import tpu as pltpu

F32 = jnp.float32
BF16 = jnp.bfloat16

D_MODEL = 1024
NORM_EPS = 1e-6
GLA_HEADS = 4
GLA_DK = 64
GLA_DV = 128
GLA_KEY_W = GLA_HEADS * GLA_DK
GLA_VAL_W = GLA_HEADS * GLA_DV
GLA_RANK = 16
GLA_TAU = 16.0
GLA_CHUNK = 64
CONV_CH = 512
CONV_WIDTH = 31
CONV_HALO = 32
DIFF_HEADS = 8
DIFF_DH = 64
DIFF_DV = 128
REL_BUCKETS = 32
REL_MAX_DIST = 128
ATT_TILE = 512
MASK_VALUE = -1e30
N_GROUPS = 4
EXPERTS_PER_GROUP = 8
N_EXPERTS = N_GROUPS * EXPERTS_PER_GROUP
D_FF = 512
MOE_TILE = 256
ROUTE_LANES = 128
PLE_DIM = 256

LANES = 128
VMEM_LIMIT_BYTES = 56 * 1024 * 1024
EVEN_Z_W = 2 * GLA_KEY_W + 2 * GLA_VAL_W + 2 * CONV_CH + LANES
EVEN_LR_BLOCK = (2 * GLA_KEY_W + 2 * GLA_VAL_W + 2 * CONV_CH) // LANES


def _params(*sem):
    return pltpu.CompilerParams(dimension_semantics=sem, vmem_limit_bytes=VMEM_LIMIT_BYTES)


def _dot(a, b):
    return jnp.dot(a, b, preferred_element_type=F32)


def _dot_nt(a, b):
    return lax.dot_general(a, b, (((1,), (1,)), ((), ())), preferred_element_type=F32)


def _dot_tn(a, b):
    return lax.dot_general(a, b, (((0,), (0,)), ((), ())), preferred_element_type=F32)


def _split_bf16(x):
    hi = x.astype(BF16)
    lo = (x - hi.astype(F32)).astype(BF16)
    return hi, lo


def _rmsnorm(x, g):
    ms = jnp.mean(x * x, axis=-1, keepdims=True)
    return x * lax.rsqrt(ms + NORM_EPS) * g


def _sigmoid(x):
    return 1.0 / (1.0 + jnp.exp(-x))


def _silu(x):
    return x * _sigmoid(x)


def _row_tile(n_rows, want):
    t = min(want, n_rows)
    assert n_rows % t == 0, (n_rows, t)
    return t


def _norm_proj_kernel(x_ref, g_ref, w_ref, o_ref):
    h = _rmsnorm(x_ref[...], g_ref[...]).astype(BF16)
    o_ref[...] = _dot(h, w_ref[...])


def norm_proj(x, g, w_bf16, tm=256):
    t, d = x.shape
    n = w_bf16.shape[1]
    tm = _row_tile(t, tm)
    return pl.pallas_call(
        _norm_proj_kernel,
        out_shape=jax.ShapeDtypeStruct((t, n), F32),
        grid=(t // tm,),
        in_specs=[pl.BlockSpec((tm, d), lambda i: (i, 0)),
                  pl.BlockSpec((1, d), lambda i: (0, 0)),
                  pl.BlockSpec((d, n), lambda i: (0, 0))],
        out_specs=pl.BlockSpec((tm, n), lambda i: (i, 0)),
        compiler_params=_params("parallel"),
        name="norm_proj",
    )(x, g.reshape(1, d), w_bf16)


def _proj_residual_kernel(*refs, n_in):
    a_refs = refs[:n_in]
    w_ref, x_ref, o_ref = refs[n_in:]
    acc = x_ref[...]
    row = 0
    for a_ref in a_refs:
        k = a_ref.shape[1]
        acc = acc + _dot(a_ref[...].astype(BF16), w_ref[row:row + k, :])
        row += k
    o_ref[...] = acc


def proj_residual(acts, w_bf16, x, tm=512):
    t, d = x.shape
    tm = _row_tile(t, tm)
    n_in = len(acts)
    in_specs = [pl.BlockSpec((tm, a.shape[1]), lambda i: (i, 0)) for a in acts]
    in_specs += [pl.BlockSpec(w_bf16.shape, lambda i: (0, 0)),
                 pl.BlockSpec((tm, d), lambda i: (i, 0))]
    return pl.pallas_call(
        functools.partial(_proj_residual_kernel, n_in=n_in),
        out_shape=jax.ShapeDtypeStruct((t, d), F32),
        grid=(t // tm,),
        in_specs=in_specs,
        out_specs=pl.BlockSpec((tm, d), lambda i: (i, 0)),
        compiler_params=_params("parallel"),
        name="proj_residual",
    )(*acts, w_bf16, x)


def _ple_kernel(x_ref, p_ref, g_ref, wg_ref, wp_ref, o_ref):
    x = x_ref[...]
    h = _rmsnorm(x, g_ref[...]).astype(BF16)
    gate = _sigmoid(_dot(h, wg_ref[...]))
    proj = _dot(p_ref[...].astype(BF16), wp_ref[...])
    o_ref[...] = x + gate * proj


def ple_add(x, p, g, wg_bf16, wp_bf16, tm=512):
    t, d = x.shape
    tm = _row_tile(t, tm)
    pd = p.shape[1]
    return pl.pallas_call(
        _ple_kernel,
        out_shape=jax.ShapeDtypeStruct((t, d), F32),
        grid=(t // tm,),
        in_specs=[pl.BlockSpec((tm, d), lambda i: (i, 0)),
                  pl.BlockSpec((tm, pd), lambda i: (i, 0)),
                  pl.BlockSpec((1, d), lambda i: (0, 0)),
                  pl.BlockSpec((d, d), lambda i: (0, 0)),
                  pl.BlockSpec((pd, d), lambda i: (0, 0))],
        out_specs=pl.BlockSpec((tm, d), lambda i: (i, 0)),
        compiler_params=_params("parallel"),
        name="ple_add",
    )(x, p, g.reshape(1, d), wg_bf16, wp_bf16)


def _gla_constants(chunk):
    c = chunk
    t = np.arange(c)
    tri = (t[None, :] <= t[:, None]).astype(np.float32)
    causal4 = np.tile(tri, (GLA_HEADS, 1))
    lane_head = np.arange(GLA_KEY_W) // GLA_DK
    row_head = np.repeat(np.arange(GLA_HEADS), c)
    headmask4 = (row_head[:, None] == lane_head[None, :]).astype(np.float32)
    blockdiag = (np.arange(GLA_KEY_W)[:, None] // GLA_DK
                 == np.arange(GLA_VAL_W)[None, :] // GLA_DV).astype(np.float32)
    return (jnp.asarray(tri, BF16), jnp.asarray(causal4), jnp.asarray(headmask4), jnp.asarray(blockdiag))


def _gla_kernel(*refs, chunk, n_chunks, has_state, valid_rows):
    if has_state:
        (qk_ref, v_ref, r_ref, lr_ref, wlr_ref, blr_ref, gg_ref, tri_ref, causal_ref, hmask_ref, bd_ref,
         s0_ref, o_ref, sout_ref, s_scr) = refs
    else:
        (qk_ref, v_ref, r_ref, lr_ref, wlr_ref, blr_ref, gg_ref, tri_ref, causal_ref, hmask_ref, bd_ref,
         o_ref, sout_ref, s_scr) = refs
    c = chunk
    i = pl.program_id(1)
    bd = bd_ref[...]

    @pl.when(i == 0)
    def _init():
        if has_state:
            s0 = s0_ref[0]
            s_scr[...] = jnp.concatenate([s0] * GLA_HEADS, axis=1) * bd
        else:
            s_scr[...] = jnp.zeros_like(s_scr)

    tri = tri_ref[...]
    ones = jnp.ones((c, LANES), BF16)

    def chunk_step(ci, carry):
        rows = pl.ds(pl.multiple_of(ci * c, c), c)
        qk = qk_ref[rows, :]
        q = qk[:, :GLA_KEY_W] * (GLA_DK ** -0.5)
        k = qk[:, GLA_KEY_W:]
        vb = v_ref[rows, :].astype(BF16)
        x = _dot(lr_ref[rows, :].astype(BF16), wlr_ref[...]) + blr_ref[...]
        g = (jnp.minimum(x, 0.0) - jnp.log1p(jnp.exp(-jnp.abs(x)))) * (1.0 / GLA_TAU)
        if valid_rows < c:
            g = jnp.where(lax.broadcasted_iota(jnp.int32, g.shape, 0) < valid_rows, g, 0.0)
        g_hi, g_lo = _split_bf16(g)
        b = _dot(tri, g_hi) + _dot(tri, g_lo)
        dcol = _dot_tn(g_hi, ones) + _dot_tn(g_lo, ones)
        bend = b[c - 1:c, :]
        half = 0.5 * bend
        qs = q * jnp.exp(b - half)
        ks = k * jnp.exp(half - b)
        qt = q * jnp.exp(b)
        kh = k * jnp.exp(bend - b)
        qstack = (jnp.concatenate([qs] * GLA_HEADS, axis=0) * hmask_ref[...]).astype(BF16)
        scores = (_dot_nt(qstack, ks.astype(BF16)) * causal_ref[...]).astype(BF16)
        s_bd = s_scr[...]
        o_inter = _dot(qt.astype(BF16), s_bd.astype(BF16))
        gg = gg_ref[...]
        for h in range(GLA_HEADS):
            cols = slice(h * GLA_DV, (h + 1) * GLA_DV)
            o_h = _dot(scores[h * c:(h + 1) * c, :], vb[:, cols]) + o_inter[:, cols]
            y = _rmsnorm(o_h, gg) * _silu(r_ref[rows, cols])
            o_ref[rows, cols] = y.astype(o_ref.dtype)
        decay = jnp.concatenate([jnp.exp(dcol)] * GLA_HEADS, axis=1)
        s_scr[...] = s_bd * decay + _dot_tn(kh.astype(BF16), vb) * bd
        return carry

    lax.fori_loop(0, n_chunks, chunk_step, 0)

    @pl.when(i == pl.num_programs(1) - 1)
    def _final():
        s_bd = s_scr[...]
        for h in range(GLA_HEADS):
            sout_ref[0, h * GLA_DK:(h + 1) * GLA_DK, :] = s_bd[h * GLA_DK:(h + 1) * GLA_DK,
                                                               h * GLA_DV:(h + 1) * GLA_DV]


def gla_mix(z, n_batch, seq, chunk, tile, w_lr_pad, b_lr, gla_g, state0, valid_rows=None):
    nt = seq // tile
    assert seq % tile == 0 and tile % chunk == 0
    valid_rows = chunk if valid_rows is None else valid_rows
    assert valid_rows == chunk or seq == chunk
    has_state = state0 is not None
    consts = _gla_constants(chunk)
    row = lambda b, i: b * nt + i
    in_specs = [pl.BlockSpec((tile, 2 * GLA_KEY_W), lambda b, i: (row(b, i), 0)),
                pl.BlockSpec((tile, GLA_VAL_W), lambda b, i: (row(b, i), 1)),
                pl.BlockSpec((tile, GLA_VAL_W), lambda b, i: (row(b, i), 2)),
                pl.BlockSpec((tile, LANES), lambda b, i: (row(b, i), EVEN_LR_BLOCK)),
                pl.BlockSpec(w_lr_pad.shape, lambda b, i: (0, 0)),
                pl.BlockSpec((1, GLA_KEY_W), lambda b, i: (0, 0)),
                pl.BlockSpec((1, GLA_DV), lambda b, i: (0, 0))]
    in_specs += [pl.BlockSpec(a.shape, lambda b, i: (0, 0)) for a in consts]
    args = [z, z, z, z, w_lr_pad, b_lr.reshape(1, GLA_KEY_W), gla_g.reshape(1, GLA_DV), *consts]
    if has_state:
        in_specs.append(pl.BlockSpec((1, GLA_KEY_W, GLA_DV), lambda b, i: (b, 0, 0)))
        args.append(state0)
    return pl.pallas_call(
        functools.partial(_gla_kernel, chunk=chunk, n_chunks=tile // chunk, has_state=has_state,
                          valid_rows=valid_rows),
        out_shape=(jax.ShapeDtypeStruct((n_batch * seq, GLA_VAL_W), BF16),
                   jax.ShapeDtypeStruct((n_batch, GLA_KEY_W, GLA_DV), F32)),
        grid=(n_batch, nt),
        in_specs=in_specs,
        out_specs=(pl.BlockSpec((tile, GLA_VAL_W), lambda b, i: (row(b, i), 0)),
                   pl.BlockSpec((1, GLA_KEY_W, GLA_DV), lambda b, i: (b, 0, 0))),
        scratch_shapes=[pltpu.VMEM((GLA_KEY_W, GLA_VAL_W), F32)],
        compiler_params=_params("parallel", "arbitrary"),
        name="gla_mix",
    )(*args)


def _layernorm_silu(c, g, b):
    mu = jnp.mean(c, axis=-1, keepdims=True)
    cc = c - mu
    var = jnp.mean(cc * cc, axis=-1, keepdims=True)
    return _silu(cc * lax.rsqrt(var + NORM_EPS) * g + b)


def _conv_kernel(a_ref, gt_ref, cw_ref, cb_ref, lg_ref, lb_ref, o_ref, st_ref, ext, *, tile, sub):
    i = pl.program_id(1)
    n_state = CONV_WIDTH - 1

    @pl.when(i == 0)
    def _zero_halo():
        ext[0:CONV_HALO, :] = jnp.zeros((CONV_HALO, CONV_CH), F32)

    @pl.when(i > 0)
    def _carry_halo():
        ext[0:CONV_HALO, :] = ext[tile:tile + CONV_HALO, :]

    ext[CONV_HALO:CONV_HALO + tile, :] = a_ref[...] * _sigmoid(gt_ref[...])
    lg = lg_ref[...]
    lb = lb_ref[...]
    for s in range(tile // sub):
        acc = jnp.broadcast_to(cb_ref[...], (sub, CONV_CH))
        for w in range(CONV_WIDTH):
            off = s * sub + (CONV_HALO - n_state) + w
            acc = acc + ext[off:off + sub, :] * cw_ref[w:w + 1, :]
        o_ref[s * sub:(s + 1) * sub, :] = _layernorm_silu(acc, lg, lb).astype(o_ref.dtype)

    @pl.when(i == pl.num_programs(1) - 1)
    def _state():
        st_ref[0] = ext[tile + CONV_HALO - n_state:tile + CONV_HALO, :]


def conv_mix(z, n_batch, seq, conv_w_pad, conv_b, ln_g, ln_b, tile=256, sub=32):
    nt = seq // tile
    assert seq % tile == 0 and tile % sub == 0 and tile >= CONV_HALO
    row = lambda b, i: b * nt + i
    vec = lambda a: a.reshape(1, CONV_CH)
    return pl.pallas_call(
        functools.partial(_conv_kernel, tile=tile, sub=sub),
        out_shape=(jax.ShapeDtypeStruct((n_batch * seq, CONV_CH), BF16),
                   jax.ShapeDtypeStruct((n_batch, CONV_WIDTH - 1, CONV_CH), F32)),
        grid=(n_batch, nt),
        in_specs=[pl.BlockSpec((tile, CONV_CH), lambda b, i: (row(b, i), 3)),
                  pl.BlockSpec((tile, CONV_CH), lambda b, i: (row(b, i), 4)),
                  pl.BlockSpec(conv_w_pad.shape, lambda b, i: (0, 0)),
                  pl.BlockSpec((1, CONV_CH), lambda b, i: (0, 0)),
                  pl.BlockSpec((1, CONV_CH), lambda b, i: (0, 0)),
                  pl.BlockSpec((1, CONV_CH), lambda b, i: (0, 0))],
        out_specs=(pl.BlockSpec((tile, CONV_CH), lambda b, i: (row(b, i), 0)),
                   pl.BlockSpec((1, CONV_WIDTH - 1, CONV_CH), lambda b, i: (b, 0, 0))),
        scratch_shapes=[pltpu.VMEM((tile + CONV_HALO, CONV_CH), F32)],
        compiler_params=_params("parallel", "arbitrary"),
        name="conv_mix",
    )(z, z, conv_w_pad, vec(conv_b), vec(ln_g), vec(ln_b))


def _conv_step_kernel(a_ref, gt_ref, st_ref, cw_ref, cb_ref, lg_ref, lb_ref, o_ref, sto_ref):
    n_state = CONV_WIDTH - 1
    u = a_ref[...] * _sigmoid(gt_ref[...])
    acc = cb_ref[...] + u * cw_ref[n_state:n_state + 1, :]
    for w in range(n_state):
        acc = acc + st_ref[:, w * CONV_CH:(w + 1) * CONV_CH] * cw_ref[w:w + 1, :]
    o_ref[...] = _layernorm_silu(acc, lg_ref[...], lb_ref[...]).astype(o_ref.dtype)
    sto_ref[:, 0:(n_state - 1) * CONV_CH] = st_ref[:, CONV_CH:n_state * CONV_CH]
    sto_ref[:, (n_state - 1) * CONV_CH:] = u


def conv_step(z, state, conv_w_pad, conv_b, ln_g, ln_b, tile=32):
    nb = z.shape[0]
    n_state = CONV_WIDTH - 1
    tile = _row_tile(nb, tile)
    vec = lambda a: a.reshape(1, CONV_CH)
    st2 = state.reshape(nb, n_state * CONV_CH)
    out, st_new = pl.pallas_call(
        _conv_step_kernel,
        out_shape=(jax.ShapeDtypeStruct((nb, CONV_CH), BF16),
                   jax.ShapeDtypeStruct((nb, n_state * CONV_CH), F32)),
        grid=(nb // tile,),
        in_specs=[pl.BlockSpec((tile, CONV_CH), lambda i: (i, 3)),
                  pl.BlockSpec((tile, CONV_CH), lambda i: (i, 4)),
                  pl.BlockSpec((tile, n_state * CONV_CH), lambda i: (i, 0)),
                  pl.BlockSpec(conv_w_pad.shape, lambda i: (0, 0)),
                  pl.BlockSpec((1, CONV_CH), lambda i: (0, 0)),
                  pl.BlockSpec((1, CONV_CH), lambda i: (0, 0)),
                  pl.BlockSpec((1, CONV_CH), lambda i: (0, 0))],
        out_specs=(pl.BlockSpec((tile, CONV_CH), lambda i: (i, 0)),
                   pl.BlockSpec((tile, n_state * CONV_CH), lambda i: (i, 0))),
        compiler_params=_params("parallel"),
        name="conv_step",
    )(z, z, st2, conv_w_pad, vec(conv_b), vec(ln_g), vec(ln_b))
    return out, st_new.reshape(nb, n_state, CONV_CH)


def _group_constants():
    grp = np.zeros((D_MODEL, LANES), np.float32)
    grp[np.arange(D_MODEL), np.arange(D_MODEL) // DIFF_DH] = 1.0
    return jnp.asarray(grp, BF16), jnp.asarray(grp.T.copy(), BF16)


def _qkv_kernel(x_ref, g_ref, w_ref, grp_ref, grpt_ref, qg_ref, kg_ref, q_ref, k_ref, v_ref, kb_ref, vb_ref):
    h = _rmsnorm(x_ref[...], g_ref[...]).astype(BF16)
    z = _dot(h, w_ref[...])
    grp = grp_ref[...]
    grpt = grpt_ref[...]

    def map_norm(y, gain):
        hi, lo = _split_bf16(y * y)
        ms = (_dot(hi, grp) + _dot(lo, grp)) * (1.0 / DIFF_DH)
        rh, rl = _split_bf16(lax.rsqrt(ms + NORM_EPS))
        return y * (_dot(rh, grpt) + _dot(rl, grpt)) * gain

    qn = map_norm(z[:, :D_MODEL], qg_ref[...])
    kn = map_norm(z[:, D_MODEL:2 * D_MODEL], kg_ref[...])
    v = z[:, 2 * D_MODEL:]
    q_ref[...] = (qn * (DIFF_DH ** -0.5)).astype(BF16)
    k_ref[...] = kn
    v_ref[...] = v
    kb_ref[...] = kn.astype(BF16)
    ones = jnp.ones((v.shape[0], DIFF_DV), BF16)
    for h in range(DIFF_HEADS):
        vb_ref[:, 2 * h * DIFF_DV:(2 * h + 1) * DIFF_DV] = v[:, h * DIFF_DV:(h + 1) * DIFF_DV].astype(BF16)
        vb_ref[:, (2 * h + 1) * DIFF_DV:(2 * h + 2) * DIFF_DV] = ones


def qkv_proj(x, g, w_bf16, q_gain, k_gain, tm=256):
    t, d = x.shape
    tm = _row_tile(t, tm)
    grp, grpt = _group_constants()
    tile_gain = lambda a: jnp.tile(a, D_MODEL // DIFF_DH).reshape(1, D_MODEL)
    row = pl.BlockSpec((tm, d), lambda i: (i, 0))
    full = lambda a: pl.BlockSpec(a.shape, lambda i: (0, 0))
    qg, kg = tile_gain(q_gain), tile_gain(k_gain)
    return pl.pallas_call(
        _qkv_kernel,
        out_shape=(jax.ShapeDtypeStruct((t, d), BF16), jax.ShapeDtypeStruct((t, d), F32),
                   jax.ShapeDtypeStruct((t, d), F32), jax.ShapeDtypeStruct((t, d), BF16),
                   jax.ShapeDtypeStruct((t, 2 * d), BF16)),
        grid=(t // tm,),
        in_specs=[row, pl.BlockSpec((1, d), lambda i: (0, 0)), full(w_bf16), full(grp), full(grpt),
                  full(qg), full(kg)],
        out_specs=(row, row, row, row, pl.BlockSpec((tm, 2 * d), lambda i: (i, 0))),
        compiler_params=_params("parallel"),
        name="qkv_proj",
    )(x, g.reshape(1, d), w_bf16, grp, grpt, qg, kg)


def _rel_bucket_np(n):
    max_exact = REL_BUCKETS // 2
    nf = np.maximum(n, 1).astype(np.float64)
    large = max_exact + (np.log(nf / max_exact) / math.log(REL_MAX_DIST / max_exact)
                         * (REL_BUCKETS - max_exact)).astype(np.int32)
    return np.where(n < max_exact, np.maximum(n, 0), np.minimum(large, REL_BUCKETS - 1))


def _diff_lambda(lamv, lam0):
    s1 = jnp.sum(lamv[0:1, :] * lamv[1:2, :], axis=-1, keepdims=True)
    s2 = jnp.sum(lamv[2:3, :] * lamv[3:4, :], axis=-1, keepdims=True)
    return jnp.exp(s1) - jnp.exp(s2) + lam0


def _attn_kernel(q_ref, k_ref, v_ref, bdiag_ref, bsub_ref, bfar_ref, lamv_ref, sub_ref, o_ref,
                 m_scr, acc_scr, *, tile, lam0):
    qi = pl.program_id(2)
    q = q_ref[...]
    lane = lax.broadcasted_iota(jnp.int32, q.shape, 1)
    zero = jnp.zeros_like(q)
    q_maps = (jnp.where(lane < DIFF_DH, q, zero), jnp.where(lane >= DIFF_DH, q, zero))
    reps = tile // LANES

    m_scr[...] = jnp.full(m_scr.shape, -jnp.inf, F32)
    acc_scr[...] = jnp.zeros(acc_scr.shape, F32)

    def flash_step(kt, bias):
        rows = pl.ds(pl.multiple_of(kt * tile, tile), tile)
        k = k_ref[rows, :]
        v = v_ref[rows, :]
        for m in range(2):
            s = _dot_nt(q_maps[m], k) + bias
            m_prev = m_scr[m]
            m_new = jnp.maximum(m_prev, jnp.max(s, axis=1, keepdims=True))
            alpha = jnp.exp(m_prev - m_new)
            p = jnp.exp(s - jnp.concatenate([m_new] * reps, axis=1))
            acc_scr[m] = jnp.concatenate([alpha, alpha], axis=1) * acc_scr[m] + _dot(p.astype(BF16), v)
            m_scr[m] = m_new

    n_far = jnp.maximum(qi - 1, 0)

    def far_pair(i, carry):
        flash_step(2 * i, bfar_ref[0])
        flash_step(2 * i + 1, bfar_ref[0])
        return carry

    lax.fori_loop(0, n_far // 2, far_pair, 0)

    @pl.when(lax.rem(n_far, 2) == 1)
    def _far_tail():
        flash_step(n_far - 1, bfar_ref[0])

    @pl.when(qi >= 1)
    def _sub_and_diag():
        flash_step(qi - 1, bsub_ref[0])
        flash_step(qi, bdiag_ref[0])

    @pl.when(qi == 0)
    def _diag_only():
        flash_step(qi, bdiag_ref[0])

    lam = _diff_lambda(lamv_ref[...], lam0)
    a0 = acc_scr[0]
    a1 = acc_scr[1]
    o = a0[:, :DIFF_DV] / a0[:, DIFF_DV:] - lam * (a1[:, :DIFF_DV] / a1[:, DIFF_DV:])
    o_ref[...] = (_rmsnorm(o, sub_ref[...]) * (1.0 - lam0)).astype(o_ref.dtype)


def _bucket_lookup(rel_bias, n):
    onehot = np.zeros((n.size, REL_BUCKETS), np.float32)
    onehot[np.arange(n.size), _rel_bucket_np(n).reshape(-1)] = 1.0
    out = jnp.dot(jnp.asarray(onehot), rel_bias.astype(F32), precision=lax.Precision.HIGHEST)
    return out.reshape(n.shape + (rel_bias.shape[1],))


def prompt_bias_tables(rel_bias, tile=ATT_TILE):
    r = np.arange(tile)
    n_diag = r[:, None] - r[None, :]
    bdiag = jnp.where(jnp.asarray(n_diag >= 0)[None], jnp.moveaxis(_bucket_lookup(rel_bias, n_diag), -1, 0),
                      MASK_VALUE)
    bsub = jnp.moveaxis(_bucket_lookup(rel_bias, n_diag + tile), -1, 0)
    assert tile + 1 >= REL_MAX_DIST
    bfar = jnp.broadcast_to(rel_bias[REL_BUCKETS - 1][:, None, None], (DIFF_HEADS, 1, tile))
    return bdiag.astype(F32), bsub.astype(F32), bfar.astype(F32)


def diff_attention(q, kb, vb1, n_batch, seq, bias_tables, lamv, subln, lam0, tile=ATT_TILE):
    nq = seq // tile
    assert seq % tile == 0
    bdiag, bsub, bfar = bias_tables
    head_tile = lambda b, h, i: (h, 0, 0)
    return pl.pallas_call(
        functools.partial(_attn_kernel, tile=tile, lam0=lam0),
        out_shape=jax.ShapeDtypeStruct((n_batch * seq, D_MODEL), BF16),
        grid=(n_batch, DIFF_HEADS, nq),
        in_specs=[pl.BlockSpec((tile, DIFF_DV), lambda b, h, i: (b * nq + i, h)),
                  pl.BlockSpec((seq, DIFF_DV), lambda b, h, i: (b, h)),
                  pl.BlockSpec((seq, 2 * DIFF_DV), lambda b, h, i: (b, h)),
                  pl.BlockSpec((1, tile, tile), head_tile),
                  pl.BlockSpec((1, tile, tile), head_tile),
                  pl.BlockSpec((1, 1, tile), head_tile),
                  pl.BlockSpec((4, DIFF_DH), lambda b, h, i: (0, 0)),
                  pl.BlockSpec((1, DIFF_DV), lambda b, h, i: (0, 0))],
        out_specs=pl.BlockSpec((tile, DIFF_DV), lambda b, h, i: (b * nq + i, h)),
        scratch_shapes=[pltpu.VMEM((2, tile, LANES), F32), pltpu.VMEM((2, tile, 2 * DIFF_DV), F32)],
        compiler_params=_params("parallel", "parallel", "arbitrary"),
        name="diff_attention",
    )(q, kb, vb1, bdiag, bsub, bfar, lamv, subln.reshape(1, DIFF_DV))


N_MAPS = 2 * DIFF_HEADS


def step_bias_tables(rel_bias, n_past):
    n = n_past - np.arange(n_past)
    past = _bucket_lookup(rel_bias, n).T
    new = jnp.broadcast_to(_bucket_lookup(rel_bias, np.zeros((1,), np.int64)).T, (DIFF_HEADS, LANES))
    return jnp.concatenate([past, past], axis=0), jnp.concatenate([new, new], axis=0)


def _step_constants(page):
    j = np.arange(N_MAPS)
    c = np.arange(D_MODEL)
    col_head, col_map = c // DIFF_DV, (c % DIFF_DV) // DIFF_DH
    sel = ((j[:, None] % DIFF_HEADS == col_head[None, :])
           & (j[:, None] // DIFF_HEADS == col_map[None, :])).astype(np.float32)
    r = np.arange(page * DIFF_HEADS)
    tok_of_row = (r[:, None] // DIFF_HEADS == np.arange(page)[None, :]).astype(np.float32)
    return jnp.asarray(sel), jnp.asarray(tok_of_row)


def _attn_step_kernel(pt_ref, q_ref, kn_ref, vn_ref, *refs, n_pages, page, lam0):
    kt_refs = refs[:n_pages]
    v_refs = refs[n_pages:2 * n_pages]
    (bias_ref, nbias_ref, sel_ref, tok_ref, lamv_ref, sub_ref, o_ref, s_scr) = refs[2 * n_pages:]
    sel = sel_ref[...]
    q_sel = sel * q_ref[0].astype(F32)
    q_bf = q_sel.astype(BF16)
    for p in range(n_pages):
        cols = slice(p * page, (p + 1) * page)
        s_scr[:, cols] = _dot(q_bf, kt_refs[p][...].astype(BF16)) + bias_ref[:, cols]
    s_new = jnp.sum(q_sel * kn_ref[0], axis=1, keepdims=True) + nbias_ref[:, 0:1]
    s = s_scr[...]
    m = jnp.maximum(jnp.max(s, axis=1, keepdims=True), s_new)
    p_all = jnp.exp(s - m)
    p_new = jnp.exp(s_new - m)
    l = jnp.sum(p_all, axis=1, keepdims=True) + p_new
    lam = _diff_lambda(lamv_ref[...], lam0)
    row = lax.broadcasted_iota(jnp.int32, (N_MAPS, 1), 0)
    coef = jnp.where(row < DIFF_HEADS, 1.0 / l, -lam / l)
    w_all = p_all * coef
    w_heads = w_all[:DIFF_HEADS, :] + w_all[DIFF_HEADS:, :]
    w_new = (p_new * coef)[:DIFF_HEADS, :] + (p_new * coef)[DIFF_HEADS:, :]
    tok_of_row = tok_ref[...]
    acc = w_new * vn_ref[0]
    for p in range(n_pages):
        w_p = w_heads[:, p * page:(p + 1) * page]
        w_rows = jnp.broadcast_to(w_p[None], (page, DIFF_HEADS, page)).reshape(page * DIFF_HEADS, page)
        w_col = jnp.sum(w_rows * tok_of_row, axis=1, keepdims=True)
        acc = acc + jnp.sum((w_col * v_refs[p][...]).reshape(page, DIFF_HEADS, DIFF_DV), axis=0)
    o_ref[0] = _rmsnorm(acc, sub_ref[...]) * (1.0 - lam0)


def diff_attention_step(q, k_new, v_new, cache_kt, cache_v2, layer, page_table, bias_tables, lamv, subln, lam0):
    nb = q.shape[0]
    n_pages = page_table.shape[1]
    page = cache_kt.shape[3]
    n_past = n_pages * page
    past_bias, new_bias = bias_tables
    sel, tok_of_row = _step_constants(page)
    tok_spec = pl.BlockSpec((1, 1, D_MODEL), lambda b, pt: (b, 0, 0))
    head_spec = pl.BlockSpec((1, DIFF_HEADS, DIFF_DV), lambda b, pt: (b, 0, 0))
    kt_spec = lambda p: pl.BlockSpec((None, None, D_MODEL, page),
                                     lambda b, pt: (layer, pt[b * n_pages + p], 0, 0))
    v_spec = lambda p: pl.BlockSpec((None, None, page * DIFF_HEADS, DIFF_DV),
                                    lambda b, pt: (layer, pt[b * n_pages + p], 0, 0))
    full = lambda a: pl.BlockSpec(a.shape, lambda b, pt: (0,) * a.ndim)
    sub2 = subln.reshape(1, DIFF_DV)
    grid_spec = pltpu.PrefetchScalarGridSpec(
        num_scalar_prefetch=1,
        grid=(nb,),
        in_specs=([tok_spec, tok_spec, head_spec]
                  + [kt_spec(p) for p in range(n_pages)] + [v_spec(p) for p in range(n_pages)]
                  + [full(past_bias), full(new_bias), full(sel), full(tok_of_row), full(lamv), full(sub2)]),
        out_specs=head_spec,
        scratch_shapes=[pltpu.VMEM((N_MAPS, n_past), F32)],
    )
    out = pl.pallas_call(
        functools.partial(_attn_step_kernel, n_pages=n_pages, page=page, lam0=lam0),
        out_shape=jax.ShapeDtypeStruct((nb, DIFF_HEADS, DIFF_DV), F32),
        grid_spec=grid_spec,
        compiler_params=_params("parallel"),
        name="diff_attention_step",
    )(page_table.reshape(-1), q.reshape(nb, 1, D_MODEL), k_new.reshape(nb, 1, D_MODEL),
      v_new.reshape(nb, DIFF_HEADS, DIFF_DV),
      *([cache_kt] * n_pages), *([cache_v2] * n_pages), past_bias, new_bias, sel, tok_of_row, lamv, sub2)
    return out.reshape(nb, D_MODEL)


ROW_CHUNKS = D_MODEL // LANES


def _router_kernel(*refs, tile_starts):
    n_groups = len(tile_starts)
    x_refs = refs[:n_groups]
    g_ref, w_ref, b_ref, h_ref, r_ref = refs[n_groups:]
    x = x_refs[0][...]
    for grp in range(1, n_groups):
        x = jnp.where(pl.program_id(0) >= tile_starts[grp], x_refs[grp][...], x)
    h = _rmsnorm(x, g_ref[...])
    tm = h.shape[0]
    for c in range(ROW_CHUNKS):
        h_ref[pl.ds(c, tm, stride=ROW_CHUNKS), :] = h[:, c * LANES:(c + 1) * LANES]
    logits = jnp.dot(h, w_ref[...], precision=lax.Precision.HIGHEST, preferred_element_type=F32) + b_ref[...]
    lane = lax.broadcasted_iota(jnp.int32, logits.shape, 1)
    neg = jnp.float32(-jnp.inf)
    big = jnp.int32(2 ** 30)

    def first_argmax(vals):
        mx = jnp.max(vals, axis=1, keepdims=True)
        idx = jnp.min(jnp.where(vals == mx, lane, big), axis=1, keepdims=True)
        return mx, idx

    lg = jnp.where(lane < N_GROUPS, logits, neg)
    g_max, grp = first_argmax(lg)
    w_grp = 1.0 / jnp.sum(jnp.exp(lg - g_max), axis=1, keepdims=True)
    e_lane = lane - N_GROUPS
    in_grp = (e_lane >= grp * EXPERTS_PER_GROUP) & (e_lane < (grp + 1) * EXPERTS_PER_GROUP)
    le = jnp.where(in_grp, logits, neg)
    m1, i1 = first_argmax(le)
    m2, i2 = first_argmax(jnp.where(lane == i1, neg, le))
    e2 = jnp.exp(m2 - m1)
    gate1 = w_grp / (1.0 + e2)
    gate2 = w_grp * e2 / (1.0 + e2)
    f = lambda a: a.astype(F32)
    r_ref[...] = jnp.where(lane == 0, f(i1 - N_GROUPS),
                           jnp.where(lane == 1, f(i2 - N_GROUPS),
                                     jnp.where(lane == 2, gate1, jnp.where(lane == 3, gate2, 0.0))))


def moe_router(xs, g, w_route, b_route, tm=128):
    d = xs[0].shape[1]
    n_tiles = [x.shape[0] // tm for x in xs]
    assert all(x.shape[0] % tm == 0 for x in xs)
    tile_starts = tuple(np.cumsum([0] + n_tiles[:-1]).tolist())
    total = sum(n_tiles) * tm

    def group_spec(start, n):
        return pl.BlockSpec((tm, d), lambda i: (jnp.clip(i - start, 0, n - 1), 0))

    return pl.pallas_call(
        functools.partial(_router_kernel, tile_starts=tile_starts),
        out_shape=(jax.ShapeDtypeStruct((total * ROW_CHUNKS, LANES), F32),
                   jax.ShapeDtypeStruct((total, ROUTE_LANES), F32)),
        grid=(sum(n_tiles),),
        in_specs=[group_spec(s, n) for s, n in zip(tile_starts, n_tiles)]
                 + [pl.BlockSpec((1, d), lambda i: (0, 0)),
                    pl.BlockSpec((d, ROUTE_LANES), lambda i: (0, 0)),
                    pl.BlockSpec((1, ROUTE_LANES), lambda i: (0, 0))],
        out_specs=(pl.BlockSpec((tm * ROW_CHUNKS, LANES), lambda i: (i, 0)),
                   pl.BlockSpec((tm, ROUTE_LANES), lambda i: (i, 0))),
        compiler_params=_params("parallel"),
        name="moe_router",
    )(*xs, g.reshape(1, d), w_route, b_route)


def _expert_kernel(be_ref, nu_ref, tok_cur_ref, tok_next_ref, h_hbm, wg_ref, wu_ref, wd_ref, o_ref,
                   x_buf, x_sem, wg_s, wu_s, wd_s, *, tb):
    b = pl.program_id(0)
    n_blk = pl.num_programs(0)
    slot = lax.rem(b, 2)

    def start_row(tok_ref, dst_slot, r):
        src_row = pl.multiple_of(tok_ref[0, r] * ROW_CHUNKS, ROW_CHUNKS)
        dst_row = pl.multiple_of(r * ROW_CHUNKS, ROW_CHUNKS)
        pltpu.make_async_copy(h_hbm.at[pl.ds(src_row, ROW_CHUNKS), :],
                              x_buf.at[dst_slot, pl.ds(dst_row, ROW_CHUNKS), :], x_sem.at[dst_slot]).start()

    def start_gather(tok_ref, dst_slot, inline):
        if inline:
            for r in range(tb):
                start_row(tok_ref, dst_slot, r)
        else:
            def row(r, carry):
                start_row(tok_ref, dst_slot, r)
                return carry
            lax.fori_loop(0, tb, row, 0, unroll=8)

    def wait_gather(dst_slot):
        pltpu.make_async_copy(h_hbm.at[pl.ds(0, tb * ROW_CHUNKS), :], x_buf.at[dst_slot],
                              x_sem.at[dst_slot]).wait()

    def load_block(src_slot):
        return jnp.concatenate([x_buf[src_slot, pl.ds(c, tb, stride=ROW_CHUNKS), :] for c in range(ROW_CHUNKS)],
                               axis=1).astype(BF16)

    @pl.when(b == 0)
    def _first():
        start_gather(tok_cur_ref, 0, inline=False)

    wait_gather(slot)
    prev = be_ref[jnp.maximum(b - 1, 0)]

    @pl.when((b == 0) | (be_ref[b] != prev))
    def _load_expert():
        wg_s[...] = wg_ref[...].astype(BF16)
        wu_s[...] = wu_ref[...].astype(BF16)
        wd_s[...] = wd_ref[...].astype(BF16)

    @pl.when(b < nu_ref[0])
    def _compute():
        start_gather(tok_next_ref, 1 - slot, inline=True)
        x = load_block(slot)
        mid = _silu(_dot(x, wg_s[...])) * _dot(x, wu_s[...])
        o_ref[...] = _dot(mid.astype(BF16), wd_s[...])

    @pl.when(b >= nu_ref[0])
    def _unused():
        start_gather(tok_next_ref, 1 - slot, inline=False)
        o_ref[...] = jnp.zeros_like(o_ref)

    @pl.when(b == n_blk - 1)
    def _drain():
        wait_gather(1 - slot)


def moe_experts(h, row_tok, blk_e, n_used, w_gate, w_up, w_down, layer, tb=MOE_TILE):
    d = D_MODEL
    n_blk = row_tok.shape[0] // tb
    tok3 = row_tok.reshape(n_blk, 1, tb)
    w_in_spec = pl.BlockSpec((None, None, d, D_FF), lambda b, be, nu: (layer, be[b], 0, 0))
    tok_spec = lambda nxt: pl.BlockSpec((None, 1, tb), lambda b, be, nu: (jnp.minimum(b + nxt, n_blk - 1), 0, 0),
                                        memory_space=pltpu.SMEM)
    grid_spec = pltpu.PrefetchScalarGridSpec(
        num_scalar_prefetch=2,
        grid=(n_blk,),
        in_specs=[tok_spec(0), tok_spec(1),
                  pl.BlockSpec(memory_space=pl.ANY),
                  w_in_spec, w_in_spec,
                  pl.BlockSpec((None, None, D_FF, d), lambda b, be, nu: (layer, be[b], 0, 0))],
        out_specs=pl.BlockSpec((tb, d), lambda b, be, nu: (b, 0)),
        scratch_shapes=[pltpu.VMEM((2, tb * ROW_CHUNKS, LANES), F32), pltpu.SemaphoreType.DMA((2,)),
                        pltpu.VMEM((d, D_FF), BF16), pltpu.VMEM((d, D_FF), BF16), pltpu.VMEM((D_FF, d), BF16)],
    )
    return pl.pallas_call(
        functools.partial(_expert_kernel, tb=tb),
        out_shape=jax.ShapeDtypeStruct((n_blk * tb, d), F32),
        grid_spec=grid_spec,
        compiler_params=_params("arbitrary"),
        name="moe_experts",
    )(blk_e, n_used, tok3, tok3, h, w_gate, w_up, w_down)


def _dispatch_plan(experts, tb):
    t, k_sel = experts.shape
    n_asg = t * k_sel
    flat_e = experts.reshape(n_asg)
    onehot = (flat_e[:, None] == jnp.arange(N_EXPERTS, dtype=jnp.int32)[None, :]).astype(jnp.int32)
    csum = jnp.cumsum(onehot, axis=0)
    counts = csum[-1]
    starts = jnp.cumsum(counts) - counts
    padded = (counts + tb - 1) // tb * tb
    pad_end = jnp.cumsum(padded)
    pad_start = pad_end - padded
    dest = jnp.sum(onehot * (csum - 1 + pad_start[None, :]), axis=1)
    n_blk = (n_asg + N_EXPERTS * (tb - 1) + tb - 1) // tb
    blk_row0 = jnp.arange(n_blk, dtype=jnp.int32) * tb
    blk_e = jnp.minimum(jnp.sum((blk_row0[:, None] >= pad_end[None, :]).astype(jnp.int32), axis=1),
                        N_EXPERTS - 1)
    order = jnp.argsort(flat_e).astype(jnp.int32)
    within = (blk_row0 - pad_start[blk_e])[:, None] + jnp.arange(tb, dtype=jnp.int32)[None, :]
    src = jnp.clip(starts[blk_e][:, None] + within, 0, n_asg - 1)
    row_tok = jnp.where(within < counts[blk_e][:, None], order[src.reshape(-1)].reshape(n_blk, tb) // k_sel, 0)
    n_used = (pad_end[-1:] // tb).astype(jnp.int32)
    return row_tok.reshape(-1), dest.reshape(t, k_sel), blk_e.astype(jnp.int32), n_used


def _combine_ple_kernel(x_ref, y0_ref, y1_ref, r_ref, p_ref, g_ref, wg_ref, wp_ref, o_ref):
    route = r_ref[...]
    x = x_ref[...] + route[:, 2:3] * y0_ref[...] + route[:, 3:4] * y1_ref[...]
    h = _rmsnorm(x, g_ref[...]).astype(BF16)
    gate = _sigmoid(_dot(h, wg_ref[...]))
    proj = _dot(p_ref[...].astype(BF16), wp_ref[...])
    o_ref[...] = x + gate * proj


def moe_combine_ple(x, y0, y1, route, row_offset, p, g, wg_bf16, wp_bf16, tm=512):
    t, d = x.shape
    tm = _row_tile(t, tm)
    assert row_offset % tm == 0
    off = row_offset // tm
    pd = p.shape[1]
    row = pl.BlockSpec((tm, d), lambda i: (i, 0))
    row_off = pl.BlockSpec((tm, d), lambda i: (i + off, 0))
    return pl.pallas_call(
        _combine_ple_kernel,
        out_shape=jax.ShapeDtypeStruct((t, d), F32),
        grid=(t // tm,),
        in_specs=[row, row_off, row_off,
                  pl.BlockSpec((tm, ROUTE_LANES), lambda i: (i + off, 0)),
                  pl.BlockSpec((tm, pd), lambda i: (i, 0)),
                  pl.BlockSpec((1, d), lambda i: (0, 0)),
                  pl.BlockSpec((d, d), lambda i: (0, 0)),
                  pl.BlockSpec((pd, d), lambda i: (0, 0))],
        out_specs=row,
        compiler_params=_params("parallel"),
        name="moe_combine_ple",
    )(x, y0, y1, route, p, g.reshape(1, d), wg_bf16, wp_bf16)


def moe_ple_add(xs, ps, layer, norm_g, rg_w, rg_b, re_w, re_b, w_gate, w_up, w_down, ple_g, wg_bf16, wp_bf16,
                tb=MOE_TILE):
    d = xs[0].shape[1]
    w_route = jnp.zeros((d, ROUTE_LANES), F32).at[:, :N_GROUPS].set(rg_w).at[:, N_GROUPS:N_GROUPS + N_EXPERTS].set(re_w)
    b_route = jnp.zeros((1, ROUTE_LANES), F32).at[0, :N_GROUPS].set(rg_b).at[0, N_GROUPS:N_GROUPS + N_EXPERTS].set(re_b)
    offsets = np.cumsum([0] + [x.shape[0] for x in xs]).tolist()
    h, route = moe_router(xs, norm_g, w_route, b_route)
    row_tok, dest, blk_e, n_used = _dispatch_plan(route[:, 0:2].astype(jnp.int32), tb)
    yb = moe_experts(h, row_tok, blk_e, n_used, w_gate, w_up, w_down, layer, tb)
    y0, y1 = yb[dest[:, 0]], yb[dest[:, 1]]
    return [moe_combine_ple(x, y0, y1, route, off, p, ple_g, wg_bf16, wp_bf16)
            for x, p, off in zip(xs, ps, offsets)]


def _lambda_init(layer):
    return 0.8 - 0.6 * math.exp(-0.3 * layer)


def _permute_even_in(w_in):
    d = w_in.shape[0]
    lr0 = 2 * GLA_KEY_W + 2 * GLA_VAL_W
    return jnp.concatenate([w_in[:, :lr0], w_in[:, lr0 + GLA_RANK:], w_in[:, lr0:lr0 + GLA_RANK],
                            jnp.zeros((d, LANES - GLA_RANK), w_in.dtype)], axis=1).astype(BF16)


def kernel(x_prompt, x_sample, cache_k, cache_v, state_gla, state_conv, page_table, p_prompt, p_sample, rel_bias, norm_mix, norm_ffn, norm_ple, w_in_ev, w_lr_up, b_lr, gla_norm, conv_w, conv_b, conv_ln_g, conv_ln_b, w_out_ev, w_qkv, q_norm, k_norm, lam_q1, lam_k1, lam_q2, lam_k2, subln, w_out_od, router_g_w, router_g_b, router_e_w, router_e_b, moe_w_gate, moe_w_up, moe_w_down, ple_proj, ple_gate):
    n_bp, seq, d = x_prompt.shape
    n_bs = x_sample.shape[0]
    depth = norm_mix.shape[0]
    n_layers_odd, n_pool, page = cache_k.shape[:3]
    xp = x_prompt.reshape(n_bp * seq, d)
    xs = x_sample.reshape(n_bs, d)
    ck = jnp.transpose(cache_k, (0, 1, 3, 4, 5, 2)).reshape(n_layers_odd, n_pool, d, page)
    cv = cache_v.reshape(n_layers_odd, n_pool, page * DIFF_HEADS, DIFF_DV)
    prompt_bias = prompt_bias_tables(rel_bias)
    step_bias = step_bias_tables(rel_bias, page_table.shape[1] * page)
    step_rows = 16
    kp_l, vp_l, ks_l, vs_l, gp_l, gs_l, cp_l, cs_l = [], [], [], [], [], [], [], []
    for i in range(depth):
        j = i // 2
        if i % 2 == 0:
            w_in = _permute_even_in(w_in_ev[j])
            w_lr_pad = jnp.zeros((LANES, GLA_KEY_W), BF16).at[:GLA_RANK].set(w_lr_up[j].astype(BF16))
            cw_pad = jnp.zeros((CONV_HALO, CONV_CH), F32).at[:CONV_WIDTH].set(conv_w[j])
            w_out = w_out_ev[j].astype(BF16)
            conv_args = (cw_pad, conv_b[j], conv_ln_g[j], conv_ln_b[j])
            zp = norm_proj(xp, norm_mix[i], w_in)
            op, g_p = gla_mix(zp, n_bp, seq, GLA_CHUNK, 512, w_lr_pad, b_lr[j], gla_norm[j], None)
            cp, c_p = conv_mix(zp, n_bp, seq, *conv_args)
            xp = proj_residual([op, cp], w_out, xp)
            zs = norm_proj(xs, norm_mix[i], w_in)
            zs_pad = jnp.zeros((n_bs, step_rows, EVEN_Z_W), F32).at[:, 0].set(zs).reshape(n_bs * step_rows, EVEN_Z_W)
            os_pad, g_s = gla_mix(zs_pad, n_bs, step_rows, step_rows, step_rows, w_lr_pad, b_lr[j], gla_norm[j],
                                  state_gla[j].reshape(n_bs, GLA_KEY_W, GLA_DV), valid_rows=1)
            os_ = os_pad.reshape(n_bs, step_rows, GLA_VAL_W)[:, 0]
            cs, c_s = conv_step(zs, state_conv[j], *conv_args)
            xs = proj_residual([os_, cs], w_out, xs)
            gp_l.append(g_p.reshape(n_bp, GLA_HEADS, GLA_DK, GLA_DV))
            gs_l.append(g_s.reshape(n_bs, GLA_HEADS, GLA_DK, GLA_DV))
            cp_l.append(c_p)
            cs_l.append(c_s)
        else:
            lam0 = _lambda_init(i)
            lamv = jnp.stack([lam_q1[j], lam_k1[j], lam_q2[j], lam_k2[j]]).astype(F32)
            wq = w_qkv[j].astype(BF16)
            w_out = w_out_od[j].astype(BF16)
            qp, kp, vp, kpb, vpb = qkv_proj(xp, norm_mix[i], wq, q_norm[j], k_norm[j])
            ap = diff_attention(qp, kpb, vpb, n_bp, seq, prompt_bias, lamv, subln[j], lam0)
            xp = proj_residual([ap], w_out, xp)
            qs, ks, vs, _, _ = qkv_proj(xs, norm_mix[i], wq, q_norm[j], k_norm[j])
            as_ = diff_attention_step(qs, ks, vs, ck, cv, j, page_table, step_bias, lamv, subln[j], lam0)
            xs = proj_residual([as_], w_out, xs)
            kp_l.append(kp.reshape(n_bp, seq, DIFF_HEADS, 2, DIFF_DH))
            vp_l.append(vp.reshape(n_bp, seq, DIFF_HEADS, DIFF_DV))
            ks_l.append(ks.reshape(n_bs, 1, DIFF_HEADS, 2, DIFF_DH))
            vs_l.append(vs.reshape(n_bs, 1, DIFF_HEADS, DIFF_DV))
        tail_args = (i, norm_ffn[i], router_g_w[i], router_g_b[i], router_e_w[i], router_e_b[i],
                     moe_w_gate, moe_w_up, moe_w_down, norm_ple[i], ple_gate[i].astype(BF16), ple_proj[i].astype(BF16))
        xp, xs = moe_ple_add([xp, xs], [p_prompt[i].reshape(n_bp * seq, PLE_DIM), p_sample[i].reshape(n_bs, PLE_DIM)],
                             *tail_args)
    return (xp.reshape(n_bp, seq, d), xs.reshape(n_bs, 1, d),
            jnp.stack(kp_l), jnp.stack(vp_l), jnp.stack(ks_l), jnp.stack(vs_l),
            jnp.stack(gp_l), jnp.stack(gs_l), jnp.stack(cp_l), jnp.stack(cs_l))
```

```python
import functools
import math

import numpy as np
import jax
import jax.numpy as jnp
from jax import lax
from jax.experimental import pallas as pl
from jax.experimental.pallas import tpu as pltpu

F32 = jnp.float32
BF16 = jnp.bfloat16

D_MODEL = 1024
NORM_EPS = 1e-6
GLA_HEADS = 4
GLA_DK = 64
GLA_DV = 128
GLA_KEY_W = GLA_HEADS * GLA_DK
GLA_VAL_W = GLA_HEADS * GLA_DV
GLA_RANK = 16
GLA_TAU = 16.0
GLA_CHUNK = 64
CONV_CH = 512
CONV_WIDTH = 31
CONV_HALO = 32
DIFF_HEADS = 8
DIFF_DH = 64
DIFF_DV = 128
REL_BUCKETS = 32
REL_MAX_DIST = 128
ATT_TILE = 512
MASK_VALUE = -1e30
N_GROUPS = 4
EXPERTS_PER_GROUP = 8
N_EXPERTS = N_GROUPS * EXPERTS_PER_GROUP
D_FF = 512
MOE_TILE = 256
ROUTE_LANES = 128
PLE_DIM = 256

LANES = 128
VMEM_LIMIT_BYTES = 56 * 1024 * 1024
EVEN_Z_W = 2 * GLA_KEY_W + 2 * GLA_VAL_W + 2 * CONV_CH + LANES
EVEN_LR_BLOCK = (2 * GLA_KEY_W + 2 * GLA_VAL_W + 2 * CONV_CH) // LANES


def _params(*sem):
    return pltpu.CompilerParams(dimension_semantics=sem, vmem_limit_bytes=VMEM_LIMIT_BYTES)


def _dot(a, b):
    return jnp.dot(a, b, preferred_element_type=F32)


def _dot_nt(a, b):
    return lax.dot_general(a, b, (((1,), (1,)), ((), ())), preferred_element_type=F32)


def _dot_tn(a, b):
    return lax.dot_general(a, b, (((0,), (0,)), ((), ())), preferred_element_type=F32)


def _split_bf16(x):
    hi = x.astype(BF16)
    lo = (x - hi.astype(F32)).astype(BF16)
    return hi, lo


def _rmsnorm(x, g):
    ms = jnp.mean(x * x, axis=-1, keepdims=True)
    return x * lax.rsqrt(ms + NORM_EPS) * g


def _sigmoid(x):
    return 1.0 / (1.0 + jnp.exp(-x))


def _silu(x):
    return x * _sigmoid(x)


def _row_tile(n_rows, want):
    t = min(want, n_rows)
    assert n_rows % t == 0, (n_rows, t)
    return t


def _norm_proj_kernel(x_ref, g_ref, w_ref, o_ref):
    h = _rmsnorm(x_ref[...], g_ref[...]).astype(BF16)
    o_ref[...] = _dot(h, w_ref[...])


def norm_proj(x, g, w_bf16, tm=256):
    t, d = x.shape
    n = w_bf16.shape[1]
    tm = _row_tile(t, tm)
    return pl.pallas_call(
        _norm_proj_kernel,
        out_shape=jax.ShapeDtypeStruct((t, n), F32),
        grid=(t // tm,),
        in_specs=[pl.BlockSpec((tm, d), lambda i: (i, 0)),
                  pl.BlockSpec((1, d), lambda i: (0, 0)),
                  pl.BlockSpec((d, n), lambda i: (0, 0))],
        out_specs=pl.BlockSpec((tm, n), lambda i: (i, 0)),
        compiler_params=_params("parallel"),
        name="norm_proj",
    )(x, g.reshape(1, d), w_bf16)


def _proj_residual_kernel(*refs, n_in):
    a_refs = refs[:n_in]
    w_ref, x_ref, o_ref = refs[n_in:]
    acc = x_ref[...]
    row = 0
    for a_ref in a_refs:
        k = a_ref.shape[1]
        acc = acc + _dot(a_ref[...].astype(BF16), w_ref[row:row + k, :])
        row += k
    o_ref[...] = acc


def proj_residual(acts, w_bf16, x, tm=512):
    t, d = x.shape
    tm = _row_tile(t, tm)
    n_in = len(acts)
    in_specs = [pl.BlockSpec((tm, a.shape[1]), lambda i: (i, 0)) for a in acts]
    in_specs += [pl.BlockSpec(w_bf16.shape, lambda i: (0, 0)),
                 pl.BlockSpec((tm, d), lambda i: (i, 0))]
    return pl.pallas_call(
        functools.partial(_proj_residual_kernel, n_in=n_in),
        out_shape=jax.ShapeDtypeStruct((t, d), F32),
        grid=(t // tm,),
        in_specs=in_specs,
        out_specs=pl.BlockSpec((tm, d), lambda i: (i, 0)),
        compiler_params=_params("parallel"),
        name="proj_residual",
    )(*acts, w_bf16, x)


def _ple_kernel(x_ref, p_ref, g_ref, wg_ref, wp_ref, o_ref):
    x = x_ref[...]
    h = _rmsnorm(x, g_ref[...]).astype(BF16)
    gate = _sigmoid(_dot(h, wg_ref[...]))
    proj = _dot(p_ref[...].astype(BF16), wp_ref[...])
    o_ref[...] = x + gate * proj


def ple_add(x, p, g, wg_bf16, wp_bf16, tm=512):
    t, d = x.shape
    tm = _row_tile(t, tm)
    pd = p.shape[1]
    return pl.pallas_call(
        _ple_kernel,
        out_shape=jax.ShapeDtypeStruct((t, d), F32),
        grid=(t // tm,),
        in_specs=[pl.BlockSpec((tm, d), lambda i: (i, 0)),
                  pl.BlockSpec((tm, pd), lambda i: (i, 0)),
                  pl.BlockSpec((1, d), lambda i: (0, 0)),
                  pl.BlockSpec((d, d), lambda i: (0, 0)),
                  pl.BlockSpec((pd, d), lambda i: (0, 0))],
        out_specs=pl.BlockSpec((tm, d), lambda i: (i, 0)),
        compiler_params=_params("parallel"),
        name="ple_add",
    )(x, p, g.reshape(1, d), wg_bf16, wp_bf16)


def _gla_constants(chunk):
    c = chunk
    t = np.arange(c)
    tri = (t[None, :] <= t[:, None]).astype(np.float32)
    causal4 = np.tile(tri, (GLA_HEADS, 1))
    lane_head = np.arange(GLA_KEY_W) // GLA_DK
    row_head = np.repeat(np.arange(GLA_HEADS), c)
    headmask4 = (row_head[:, None] == lane_head[None, :]).astype(np.float32)
    blockdiag = (np.arange(GLA_KEY_W)[:, None] // GLA_DK
                 == np.arange(GLA_VAL_W)[None, :] // GLA_DV).astype(np.float32)
    return (jnp.asarray(tri, BF16), jnp.asarray(causal4), jnp.asarray(headmask4), jnp.asarray(blockdiag))


def _gla_kernel(*refs, chunk, n_chunks, has_state, valid_rows, chunk_is_sequence):
    if has_state:
        (qk_ref, v_ref, r_ref, lr_ref, wlr_ref, blr_ref, gg_ref, tri_ref, causal_ref, hmask_ref, bd_ref,
         s0_ref, o_ref, sout_ref, s_scr) = refs
    else:
        (qk_ref, v_ref, r_ref, lr_ref, wlr_ref, blr_ref, gg_ref, tri_ref, causal_ref, hmask_ref, bd_ref,
         o_ref, sout_ref, s_scr) = refs
    c = chunk
    i = pl.program_id(1)
    bd = bd_ref[...]

    def to_block_diag(s):
        return jnp.concatenate([s] * GLA_HEADS, axis=1) * bd

    def store_state(s_bd, slot):
        for h in range(GLA_HEADS):
            sout_ref[slot, h * GLA_DK:(h + 1) * GLA_DK, :] = s_bd[h * GLA_DK:(h + 1) * GLA_DK,
                                                                  h * GLA_DV:(h + 1) * GLA_DV]

    if not chunk_is_sequence:
        @pl.when(i == 0)
        def _init():
            if has_state:
                s_scr[...] = to_block_diag(s0_ref[0])
            else:
                s_scr[...] = jnp.zeros_like(s_scr)

    tri = tri_ref[...]
    ones = jnp.ones((c, LANES), BF16)

    def chunk_step(ci, carry):
        rows = pl.ds(pl.multiple_of(ci * c, c), c)
        qk = qk_ref[rows, :]
        q = qk[:, :GLA_KEY_W] * (GLA_DK ** -0.5)
        k = qk[:, GLA_KEY_W:]
        vb = v_ref[rows, :].astype(BF16)
        x = _dot(lr_ref[rows, :].astype(BF16), wlr_ref[...]) + blr_ref[...]
        g = (jnp.minimum(x, 0.0) - jnp.log1p(jnp.exp(-jnp.abs(x)))) * (1.0 / GLA_TAU)
        if valid_rows < c:
            g = jnp.where(lax.broadcasted_iota(jnp.int32, g.shape, 0) < valid_rows, g, 0.0)
        g_hi, g_lo = _split_bf16(g)
        b = _dot(tri, g_hi) + _dot(tri, g_lo)
        dcol = _dot_tn(g_hi, ones) + _dot_tn(g_lo, ones)
        bend = b[c - 1:c, :]
        half = 0.5 * bend
        qs = q * jnp.exp(b - half)
        ks = k * jnp.exp(half - b)
        qt = q * jnp.exp(b)
        kh = k * jnp.exp(bend - b)
        qstack = (jnp.concatenate([qs] * GLA_HEADS, axis=0) * hmask_ref[...]).astype(BF16)
        scores = (_dot_nt(qstack, ks.astype(BF16)) * causal_ref[...]).astype(BF16)
        s_bd = to_block_diag(s0_ref[ci]) if chunk_is_sequence else s_scr[...]
        o_inter = _dot(qt.astype(BF16), s_bd.astype(BF16))
        gg = gg_ref[...]
        for h in range(GLA_HEADS):
            cols = slice(h * GLA_DV, (h + 1) * GLA_DV)
            o_h = _dot(scores[h * c:(h + 1) * c, :], vb[:, cols]) + o_inter[:, cols]
            y = _rmsnorm(o_h, gg) * _silu(r_ref[rows, cols])
            o_ref[rows, cols] = y.astype(o_ref.dtype)
        decay = jnp.concatenate([jnp.exp(dcol)] * GLA_HEADS, axis=1)
        s_new = s_bd * decay + _dot_tn(kh.astype(BF16), vb) * bd
        if chunk_is_sequence:
            store_state(s_new, ci)
        else:
            s_scr[...] = s_new
        return carry

    lax.fori_loop(0, n_chunks, chunk_step, 0)

    if not chunk_is_sequence:
        @pl.when(i == pl.num_programs(1) - 1)
        def _final():
            store_state(s_scr[...], 0)


def gla_mix(z, n_batch, seq, chunk, tile, w_lr_pad, b_lr, gla_g, state0, valid_rows=None):
    chunk_is_sequence = seq == chunk and tile > seq
    if chunk_is_sequence:
        assert state0 is not None and n_batch % (tile // seq) == 0
        seqs = tile // seq
        n_groups, nt = n_batch // seqs, 1
    else:
        assert seq % tile == 0 and tile % chunk == 0
        seqs, n_groups, nt = 1, n_batch, seq // tile
    valid_rows = chunk if valid_rows is None else valid_rows
    assert valid_rows == chunk or seq == chunk
    has_state = state0 is not None
    consts = _gla_constants(chunk)
    row = lambda b, i: b * nt + i
    in_specs = [pl.BlockSpec((tile, 2 * GLA_KEY_W), lambda b, i: (row(b, i), 0)),
                pl.BlockSpec((tile, GLA_VAL_W), lambda b, i: (row(b, i), 1)),
                pl.BlockSpec((tile, GLA_VAL_W), lambda b, i: (row(b, i), 2)),
                pl.BlockSpec((tile, LANES), lambda b, i: (row(b, i), EVEN_LR_BLOCK)),
                pl.BlockSpec(w_lr_pad.shape, lambda b, i: (0, 0)),
                pl.BlockSpec((1, GLA_KEY_W), lambda b, i: (0, 0)),
                pl.BlockSpec((1, GLA_DV), lambda b, i: (0, 0))]
    in_specs += [pl.BlockSpec(a.shape, lambda b, i: (0, 0)) for a in consts]
    args = [z, z, z, z, w_lr_pad, b_lr.reshape(1, GLA_KEY_W), gla_g.reshape(1, GLA_DV), *consts]
    state_spec = pl.BlockSpec((seqs, GLA_KEY_W, GLA_DV), lambda b, i: (b, 0, 0))
    if has_state:
        in_specs.append(state_spec)
        args.append(state0)
    return pl.pallas_call(
        functools.partial(_gla_kernel, chunk=chunk, n_chunks=tile // chunk, has_state=has_state,
                          valid_rows=valid_rows, chunk_is_sequence=chunk_is_sequence),
        out_shape=(jax.ShapeDtypeStruct((n_batch * seq, GLA_VAL_W), BF16),
                   jax.ShapeDtypeStruct((n_batch, GLA_KEY_W, GLA_DV), F32)),
        grid=(n_groups, nt),
        in_specs=in_specs,
        out_specs=(pl.BlockSpec((tile, GLA_VAL_W), lambda b, i: (row(b, i), 0)), state_spec),
        scratch_shapes=[pltpu.VMEM((GLA_KEY_W, GLA_VAL_W), F32)],
        compiler_params=_params("parallel", "arbitrary"),
        name="gla_mix",
    )(*args)


def _layernorm_silu(c, g, b):
    mu = jnp.mean(c, axis=-1, keepdims=True)
    cc = c - mu
    var = jnp.mean(cc * cc, axis=-1, keepdims=True)
    return _silu(cc * lax.rsqrt(var + NORM_EPS) * g + b)


def _conv_kernel(a_ref, gt_ref, cw_ref, cb_ref, lg_ref, lb_ref, o_ref, st_ref, ext, shifted, *, tile, sub):
    i = pl.program_id(1)
    n_state = CONV_WIDTH - 1
    sublanes = 8
    span = tile + CONV_HALO - sublanes

    @pl.when(i == 0)
    def _zero_halo():
        ext[0:CONV_HALO, :] = jnp.zeros((CONV_HALO, CONV_CH), F32)

    @pl.when(i > 0)
    def _carry_halo():
        ext[0:CONV_HALO, :] = ext[tile:tile + CONV_HALO, :]

    ext[CONV_HALO:CONV_HALO + tile, :] = a_ref[...] * _sigmoid(gt_ref[...])
    for ph in range(1, sublanes):
        shifted[ph, 0:span, :] = ext[ph:ph + span, :]
    lg = lg_ref[...]
    lb = lb_ref[...]
    for s in range(tile // sub):
        acc = jnp.broadcast_to(cb_ref[...], (sub, CONV_CH))
        for w in range(CONV_WIDTH):
            off = (CONV_HALO - n_state) + w
            ph, base = off % sublanes, s * sub + off - off % sublanes
            rows = ext[base:base + sub, :] if ph == 0 else shifted[ph, base:base + sub, :]
            acc = acc + rows * cw_ref[w:w + 1, :]
        o_ref[s * sub:(s + 1) * sub, :] = _layernorm_silu(acc, lg, lb).astype(o_ref.dtype)

    @pl.when(i == pl.num_programs(1) - 1)
    def _state():
        st_ref[0] = ext[tile + CONV_HALO - n_state:tile + CONV_HALO, :]


def conv_mix(z, n_batch, seq, conv_w_pad, conv_b, ln_g, ln_b, tile=256, sub=32):
    nt = seq // tile
    assert seq % tile == 0 and tile % sub == 0 and tile >= CONV_HALO
    row = lambda b, i: b * nt + i
    vec = lambda a: a.reshape(1, CONV_CH)
    return pl.pallas_call(
        functools.partial(_conv_kernel, tile=tile, sub=sub),
        out_shape=(jax.ShapeDtypeStruct((n_batch * seq, CONV_CH), BF16),
                   jax.ShapeDtypeStruct((n_batch, CONV_WIDTH - 1, CONV_CH), F32)),
        grid=(n_batch, nt),
        in_specs=[pl.BlockSpec((tile, CONV_CH), lambda b, i: (row(b, i), 3)),
                  pl.BlockSpec((tile, CONV_CH), lambda b, i: (row(b, i), 4)),
                  pl.BlockSpec(conv_w_pad.shape, lambda b, i: (0, 0)),
                  pl.BlockSpec((1, CONV_CH), lambda b, i: (0, 0)),
                  pl.BlockSpec((1, CONV_CH), lambda b, i: (0, 0)),
                  pl.BlockSpec((1, CONV_CH), lambda b, i: (0, 0))],
        out_specs=(pl.BlockSpec((tile, CONV_CH), lambda b, i: (row(b, i), 0)),
                   pl.BlockSpec((1, CONV_WIDTH - 1, CONV_CH), lambda b, i: (b, 0, 0))),
        scratch_shapes=[pltpu.VMEM((tile + CONV_HALO, CONV_CH), F32),
                        pltpu.VMEM((8, tile + CONV_HALO, CONV_CH), F32)],
        compiler_params=_params("parallel", "arbitrary"),
        name="conv_mix",
    )(z, z, conv_w_pad, vec(conv_b), vec(ln_g), vec(ln_b))


def _conv_step_kernel(a_ref, gt_ref, st_ref, cw_ref, cb_ref, lg_ref, lb_ref, o_ref, sto_ref):
    n_state = CONV_WIDTH - 1
    u = a_ref[...] * _sigmoid(gt_ref[...])
    acc = cb_ref[...] + u * cw_ref[n_state:n_state + 1, :]
    for w in range(n_state):
        acc = acc + st_ref[:, w * CONV_CH:(w + 1) * CONV_CH] * cw_ref[w:w + 1, :]
    o_ref[...] = _layernorm_silu(acc, lg_ref[...], lb_ref[...]).astype(o_ref.dtype)
    sto_ref[:, 0:(n_state - 1) * CONV_CH] = st_ref[:, CONV_CH:n_state * CONV_CH]
    sto_ref[:, (n_state - 1) * CONV_CH:] = u


def conv_step(z, state, conv_w_pad, conv_b, ln_g, ln_b, tile=32):
    nb = z.shape[0]
    n_state = CONV_WIDTH - 1
    tile = _row_tile(nb, tile)
    vec = lambda a: a.reshape(1, CONV_CH)
    st2 = state.reshape(nb, n_state * CONV_CH)
    out, st_new = pl.pallas_call(
        _conv_step_kernel,
        out_shape=(jax.ShapeDtypeStruct((nb, CONV_CH), BF16),
                   jax.ShapeDtypeStruct((nb, n_state * CONV_CH), F32)),
        grid=(nb // tile,),
        in_specs=[pl.BlockSpec((tile, CONV_CH), lambda i: (i, 3)),
                  pl.BlockSpec((tile, CONV_CH), lambda i: (i, 4)),
                  pl.BlockSpec((tile, n_state * CONV_CH), lambda i: (i, 0)),
                  pl.BlockSpec(conv_w_pad.shape, lambda i: (0, 0)),
                  pl.BlockSpec((1, CONV_CH), lambda i: (0, 0)),
                  pl.BlockSpec((1, CONV_CH), lambda i: (0, 0)),
                  pl.BlockSpec((1, CONV_CH), lambda i: (0, 0))],
        out_specs=(pl.BlockSpec((tile, CONV_CH), lambda i: (i, 0)),
                   pl.BlockSpec((tile, n_state * CONV_CH), lambda i: (i, 0))),
        compiler_params=_params("parallel"),
        name="conv_step",
    )(z, z, st2, conv_w_pad, vec(conv_b), vec(ln_g), vec(ln_b))
    return out, st_new.reshape(nb, n_state, CONV_CH)


def _group_constants():
    grp = np.zeros((D_MODEL, LANES), np.float32)
    grp[np.arange(D_MODEL), np.arange(D_MODEL) // DIFF_DH] = 1.0
    return jnp.asarray(grp, BF16), jnp.asarray(grp.T.copy(), BF16)


def _qkv_kernel(x_ref, g_ref, w_ref, grp_ref, grpt_ref, qg_ref, kg_ref, q_ref, k_ref, v_ref, kb_ref, vb_ref):
    h = _rmsnorm(x_ref[...], g_ref[...]).astype(BF16)
    z = _dot(h, w_ref[...])
    grp = grp_ref[...]
    grpt = grpt_ref[...]

    def map_norm(y, gain):
        hi, lo = _split_bf16(y * y)
        ms = (_dot(hi, grp) + _dot(lo, grp)) * (1.0 / DIFF_DH)
        rh, rl = _split_bf16(lax.rsqrt(ms + NORM_EPS))
        return y * (_dot(rh, grpt) + _dot(rl, grpt)) * gain

    qn = map_norm(z[:, :D_MODEL], qg_ref[...])
    kn = map_norm(z[:, D_MODEL:2 * D_MODEL], kg_ref[...])
    v = z[:, 2 * D_MODEL:]
    q_ref[...] = (qn * (DIFF_DH ** -0.5)).astype(BF16)
    k_ref[...] = kn
    v_ref[...] = v
    kb_ref[...] = kn.astype(BF16)
    ones = jnp.ones((v.shape[0], DIFF_DV), BF16)
    for h in range(DIFF_HEADS):
        vb_ref[:, 2 * h * DIFF_DV:(2 * h + 1) * DIFF_DV] = v[:, h * DIFF_DV:(h + 1) * DIFF_DV].astype(BF16)
        vb_ref[:, (2 * h + 1) * DIFF_DV:(2 * h + 2) * DIFF_DV] = ones


def qkv_proj(x, g, w_bf16, q_gain, k_gain, tm=256):
    t, d = x.shape
    tm = _row_tile(t, tm)
    grp, grpt = _group_constants()
    tile_gain = lambda a: jnp.tile(a, D_MODEL // DIFF_DH).reshape(1, D_MODEL)
    row = pl.BlockSpec((tm, d), lambda i: (i, 0))
    full = lambda a: pl.BlockSpec(a.shape, lambda i: (0, 0))
    qg, kg = tile_gain(q_gain), tile_gain(k_gain)
    return pl.pallas_call(
        _qkv_kernel,
        out_shape=(jax.ShapeDtypeStruct((t, d), BF16), jax.ShapeDtypeStruct((t, d), F32),
                   jax.ShapeDtypeStruct((t, d), F32), jax.ShapeDtypeStruct((t, d), BF16),
                   jax.ShapeDtypeStruct((t, 2 * d), BF16)),
        grid=(t // tm,),
        in_specs=[row, pl.BlockSpec((1, d), lambda i: (0, 0)), full(w_bf16), full(grp), full(grpt),
                  full(qg), full(kg)],
        out_specs=(row, row, row, row, pl.BlockSpec((tm, 2 * d), lambda i: (i, 0))),
        compiler_params=_params("parallel"),
        name="qkv_proj",
    )(x, g.reshape(1, d), w_bf16, grp, grpt, qg, kg)


def _rel_bucket_np(n):
    max_exact = REL_BUCKETS // 2
    nf = np.maximum(n, 1).astype(np.float64)
    large = max_exact + (np.log(nf / max_exact) / math.log(REL_MAX_DIST / max_exact)
                         * (REL_BUCKETS - max_exact)).astype(np.int32)
    return np.where(n < max_exact, np.maximum(n, 0), np.minimum(large, REL_BUCKETS - 1))


def _diff_lambda(lamv, lam0):
    s1 = jnp.sum(lamv[0:1, :] * lamv[1:2, :], axis=-1, keepdims=True)
    s2 = jnp.sum(lamv[2:3, :] * lamv[3:4, :], axis=-1, keepdims=True)
    return jnp.exp(s1) - jnp.exp(s2) + lam0


def _attn_kernel(q_ref, k_ref, v_ref, bdiag_ref, bsub_ref, bfar_ref, lamv_ref, sub_ref, o_ref,
                 m_scr, acc_scr, *, tile, lam0):
    qi = pl.program_id(2)
    q = q_ref[...]
    lane = lax.broadcasted_iota(jnp.int32, q.shape, 1)
    zero = jnp.zeros_like(q)
    q_maps = (jnp.where(lane < DIFF_DH, q, zero), jnp.where(lane >= DIFF_DH, q, zero))
    reps = tile // LANES

    m_scr[...] = jnp.full(m_scr.shape, -jnp.inf, F32)
    acc_scr[...] = jnp.zeros(acc_scr.shape, F32)

    def flash_step(kt, bias):
        rows = pl.ds(pl.multiple_of(kt * tile, tile), tile)
        k = k_ref[rows, :]
        v = v_ref[rows, :]
        for m in range(2):
            s = _dot_nt(q_maps[m], k) + bias
            m_prev = m_scr[m]
            m_new = jnp.maximum(m_prev, jnp.max(s, axis=1, keepdims=True))
            alpha = jnp.exp(m_prev - m_new)
            p = jnp.exp(s - jnp.concatenate([m_new] * reps, axis=1))
            acc_scr[m] = jnp.concatenate([alpha, alpha], axis=1) * acc_scr[m] + _dot(p.astype(BF16), v)
            m_scr[m] = m_new

    n_far = jnp.maximum(qi - 1, 0)

    def far_pair(i, carry):
        flash_step(2 * i, bfar_ref[0])
        flash_step(2 * i + 1, bfar_ref[0])
        return carry

    lax.fori_loop(0, n_far // 2, far_pair, 0)

    @pl.when(lax.rem(n_far, 2) == 1)
    def _far_tail():
        flash_step(n_far - 1, bfar_ref[0])

    @pl.when(qi >= 1)
    def _sub_and_diag():
        flash_step(qi - 1, bsub_ref[0])
        flash_step(qi, bdiag_ref[0])

    @pl.when(qi == 0)
    def _diag_only():
        flash_step(qi, bdiag_ref[0])

    lam = _diff_lambda(lamv_ref[...], lam0)
    a0 = acc_scr[0]
    a1 = acc_scr[1]
    o = a0[:, :DIFF_DV] / a0[:, DIFF_DV:] - lam * (a1[:, :DIFF_DV] / a1[:, DIFF_DV:])
    o_ref[...] = (_rmsnorm(o, sub_ref[...]) * (1.0 - lam0)).astype(o_ref.dtype)


def _bucket_lookup(rel_bias, n):
    onehot = np.zeros((n.size, REL_BUCKETS), np.float32)
    onehot[np.arange(n.size), _rel_bucket_np(n).reshape(-1)] = 1.0
    out = jnp.dot(jnp.asarray(onehot), rel_bias.astype(F32), precision=lax.Precision.HIGHEST)
    return out.reshape(n.shape + (rel_bias.shape[1],))


def prompt_bias_tables(rel_bias, tile=ATT_TILE):
    r = np.arange(tile)
    n_diag = r[:, None] - r[None, :]
    bdiag = jnp.where(jnp.asarray(n_diag >= 0)[None], jnp.moveaxis(_bucket_lookup(rel_bias, n_diag), -1, 0),
                      MASK_VALUE)
    bsub = jnp.moveaxis(_bucket_lookup(rel_bias, n_diag + tile), -1, 0)
    assert tile + 1 >= REL_MAX_DIST
    bfar = jnp.broadcast_to(rel_bias[REL_BUCKETS - 1][:, None, None], (DIFF_HEADS, 1, tile))
    return bdiag.astype(F32), bsub.astype(F32), bfar.astype(F32)


def diff_attention(q, kb, vb1, n_batch, seq, bias_tables, lamv, subln, lam0, tile=ATT_TILE):
    nq = seq // tile
    assert seq % tile == 0
    bdiag, bsub, bfar = bias_tables
    head_tile = lambda b, h, i: (h, 0, 0)
    return pl.pallas_call(
        functools.partial(_attn_kernel, tile=tile, lam0=lam0),
        out_shape=jax.ShapeDtypeStruct((n_batch * seq, D_MODEL), BF16),
        grid=(n_batch, DIFF_HEADS, nq),
        in_specs=[pl.BlockSpec((tile, DIFF_DV), lambda b, h, i: (b * nq + i, h)),
                  pl.BlockSpec((seq, DIFF_DV), lambda b, h, i: (b, h)),
                  pl.BlockSpec((seq, 2 * DIFF_DV), lambda b, h, i: (b, h)),
                  pl.BlockSpec((1, tile, tile), head_tile),
                  pl.BlockSpec((1, tile, tile), head_tile),
                  pl.BlockSpec((1, 1, tile), head_tile),
                  pl.BlockSpec((4, DIFF_DH), lambda b, h, i: (0, 0)),
                  pl.BlockSpec((1, DIFF_DV), lambda b, h, i: (0, 0))],
        out_specs=pl.BlockSpec((tile, DIFF_DV), lambda b, h, i: (b * nq + i, h)),
        scratch_shapes=[pltpu.VMEM((2, tile, LANES), F32), pltpu.VMEM((2, tile, 2 * DIFF_DV), F32)],
        compiler_params=_params("parallel", "parallel", "arbitrary"),
        name="diff_attention",
    )(q, kb, vb1, bdiag, bsub, bfar, lamv, subln.reshape(1, DIFF_DV))


N_MAPS = 2 * DIFF_HEADS


def step_bias_tables(rel_bias, n_past):
    n = n_past - np.arange(n_past)
    past = _bucket_lookup(rel_bias, n).T
    new = jnp.broadcast_to(_bucket_lookup(rel_bias, np.zeros((1,), np.int64)).T, (DIFF_HEADS, LANES))
    return jnp.concatenate([past, past], axis=0), jnp.concatenate([new, new], axis=0)


def _step_constants(page):
    j = np.arange(N_MAPS)
    c = np.arange(D_MODEL)
    col_head, col_map = c // DIFF_DV, (c % DIFF_DV) // DIFF_DH
    sel = ((j[:, None] % DIFF_HEADS == col_head[None, :])
           & (j[:, None] // DIFF_HEADS == col_map[None, :])).astype(np.float32)
    r = np.arange(page * DIFF_HEADS)
    tok_of_row = (r[:, None] // DIFF_HEADS == np.arange(page)[None, :]).astype(np.float32)
    return jnp.asarray(sel), jnp.asarray(tok_of_row)


def _attn_step_kernel(pt_ref, q_ref, kn_ref, vn_ref, *refs, n_pages, page, lam0):
    kt_refs = refs[:n_pages]
    v_refs = refs[n_pages:2 * n_pages]
    (bias_ref, nbias_ref, sel_ref, tok_ref, lamv_ref, sub_ref, o_ref, s_scr) = refs[2 * n_pages:]
    sel = sel_ref[...]
    q_sel = sel * q_ref[0].astype(F32)
    q_bf = q_sel.astype(BF16)
    for p in range(n_pages):
        cols = slice(p * page, (p + 1) * page)
        s_scr[:, cols] = _dot(q_bf, kt_refs[p][...].astype(BF16)) + bias_ref[:, cols]
    s_new = jnp.sum(q_sel * kn_ref[0], axis=1, keepdims=True) + nbias_ref[:, 0:1]
    s = s_scr[...]
    m = jnp.maximum(jnp.max(s, axis=1, keepdims=True), s_new)
    p_all = jnp.exp(s - m)
    p_new = jnp.exp(s_new - m)
    l = jnp.sum(p_all, axis=1, keepdims=True) + p_new
    lam = _diff_lambda(lamv_ref[...], lam0)
    row = lax.broadcasted_iota(jnp.int32, (N_MAPS, 1), 0)
    coef = jnp.where(row < DIFF_HEADS, 1.0 / l, -lam / l)
    w_all = p_all * coef
    w_heads = w_all[:DIFF_HEADS, :] + w_all[DIFF_HEADS:, :]
    w_new = (p_new * coef)[:DIFF_HEADS, :] + (p_new * coef)[DIFF_HEADS:, :]
    tok_of_row = tok_ref[...]
    acc = w_new * vn_ref[0]
    for p in range(n_pages):
        w_p = w_heads[:, p * page:(p + 1) * page]
        w_rows = jnp.broadcast_to(w_p[None], (page, DIFF_HEADS, page)).reshape(page * DIFF_HEADS, page)
        w_col = jnp.sum(w_rows * tok_of_row, axis=1, keepdims=True)
        acc = acc + jnp.sum((w_col * v_refs[p][...]).reshape(page, DIFF_HEADS, DIFF_DV), axis=0)
    o_ref[0] = _rmsnorm(acc, sub_ref[...]) * (1.0 - lam0)


def diff_attention_step(q, k_new, v_new, cache_kt, cache_v2, layer, page_table, bias_tables, lamv, subln, lam0):
    nb = q.shape[0]
    n_pages = page_table.shape[1]
    page = cache_kt.shape[3]
    n_past = n_pages * page
    past_bias, new_bias = bias_tables
    sel, tok_of_row = _step_constants(page)
    tok_spec = pl.BlockSpec((1, 1, D_MODEL), lambda b, pt: (b, 0, 0))
    head_spec = pl.BlockSpec((1, DIFF_HEADS, DIFF_DV), lambda b, pt: (b, 0, 0))
    kt_spec = lambda p: pl.BlockSpec((None, None, D_MODEL, page),
                                     lambda b, pt: (layer, pt[b * n_pages + p], 0, 0))
    v_spec = lambda p: pl.BlockSpec((None, None, page * DIFF_HEADS, DIFF_DV),
                                    lambda b, pt: (layer, pt[b * n_pages + p], 0, 0))
    full = lambda a: pl.BlockSpec(a.shape, lambda b, pt: (0,) * a.ndim)
    sub2 = subln.reshape(1, DIFF_DV)
    grid_spec = pltpu.PrefetchScalarGridSpec(
        num_scalar_prefetch=1,
        grid=(nb,),
        in_specs=([tok_spec, tok_spec, head_spec]
                  + [kt_spec(p) for p in range(n_pages)] + [v_spec(p) for p in range(n_pages)]
                  + [full(past_bias), full(new_bias), full(sel), full(tok_of_row), full(lamv), full(sub2)]),
        out_specs=head_spec,
        scratch_shapes=[pltpu.VMEM((N_MAPS, n_past), F32)],
    )
    out = pl.pallas_call(
        functools.partial(_attn_step_kernel, n_pages=n_pages, page=page, lam0=lam0),
        out_shape=jax.ShapeDtypeStruct((nb, DIFF_HEADS, DIFF_DV), F32),
        grid_spec=grid_spec,
        compiler_params=_params("parallel"),
        name="diff_attention_step",
    )(page_table.reshape(-1), q.reshape(nb, 1, D_MODEL), k_new.reshape(nb, 1, D_MODEL),
      v_new.reshape(nb, DIFF_HEADS, DIFF_DV),
      *([cache_kt] * n_pages), *([cache_v2] * n_pages), past_bias, new_bias, sel, tok_of_row, lamv, sub2)
    return out.reshape(nb, D_MODEL)


ROW_CHUNKS = D_MODEL // LANES


def _router_kernel(*refs, tile_starts):
    n_groups = len(tile_starts)
    x_refs = refs[:n_groups]
    g_ref, w_ref, b_ref, h_ref, r_ref = refs[n_groups:]
    x = x_refs[0][...]
    for grp in range(1, n_groups):
        x = jnp.where(pl.program_id(0) >= tile_starts[grp], x_refs[grp][...], x)
    h = _rmsnorm(x, g_ref[...])
    tm = h.shape[0]
    for c in range(ROW_CHUNKS):
        h_ref[pl.ds(c, tm, stride=ROW_CHUNKS), :] = h[:, c * LANES:(c + 1) * LANES]
    logits = jnp.dot(h, w_ref[...], precision=lax.Precision.HIGHEST, preferred_element_type=F32) + b_ref[...]
    lane = lax.broadcasted_iota(jnp.int32, logits.shape, 1)
    neg = jnp.float32(-jnp.inf)
    big = jnp.int32(2 ** 30)

    def first_argmax(vals):
        mx = jnp.max(vals, axis=1, keepdims=True)
        idx = jnp.min(jnp.where(vals == mx, lane, big), axis=1, keepdims=True)
        return mx, idx

    lg = jnp.where(lane < N_GROUPS, logits, neg)
    g_max, grp = first_argmax(lg)
    w_grp = 1.0 / jnp.sum(jnp.exp(lg - g_max), axis=1, keepdims=True)
    e_lane = lane - N_GROUPS
    in_grp = (e_lane >= grp * EXPERTS_PER_GROUP) & (e_lane < (grp + 1) * EXPERTS_PER_GROUP)
    le = jnp.where(in_grp, logits, neg)
    m1, i1 = first_argmax(le)
    m2, i2 = first_argmax(jnp.where(lane == i1, neg, le))
    e2 = jnp.exp(m2 - m1)
    gate1 = w_grp / (1.0 + e2)
    gate2 = w_grp * e2 / (1.0 + e2)
    f = lambda a: a.astype(F32)
    r_ref[...] = jnp.where(lane == 0, f(i1 - N_GROUPS),
                           jnp.where(lane == 1, f(i2 - N_GROUPS),
                                     jnp.where(lane == 2, gate1, jnp.where(lane == 3, gate2, 0.0))))


def moe_router(xs, g, w_route, b_route, tm=512):
    d = xs[0].shape[1]
    xs = [x if x.shape[0] % tm == 0 else jnp.pad(x, ((0, -x.shape[0] % tm), (0, 0))) for x in xs]
    n_tiles = [x.shape[0] // tm for x in xs]
    tile_starts = tuple(np.cumsum([0] + n_tiles[:-1]).tolist())
    total = sum(n_tiles) * tm

    def group_spec(start, n):
        return pl.BlockSpec((tm, d), lambda i: (jnp.clip(i - start, 0, n - 1), 0))

    h, route = pl.pallas_call(
        functools.partial(_router_kernel, tile_starts=tile_starts),
        out_shape=(jax.ShapeDtypeStruct((total * ROW_CHUNKS, LANES), F32),
                   jax.ShapeDtypeStruct((total, ROUTE_LANES), F32)),
        grid=(sum(n_tiles),),
        in_specs=[group_spec(s, n) for s, n in zip(tile_starts, n_tiles)]
                 + [pl.BlockSpec((1, d), lambda i: (0, 0)),
                    pl.BlockSpec((d, ROUTE_LANES), lambda i: (0, 0)),
                    pl.BlockSpec((1, ROUTE_LANES), lambda i: (0, 0))],
        out_specs=(pl.BlockSpec((tm * ROW_CHUNKS, LANES), lambda i: (i, 0)),
                   pl.BlockSpec((tm, ROUTE_LANES), lambda i: (i, 0))),
        compiler_params=_params("parallel"),
        name="moe_router",
    )(*xs, g.reshape(1, d), w_route, b_route)
    return h, route, [s * tm for s in tile_starts]


def _expert_kernel(be_ref, nu_ref, tok_cur_ref, tok_next_ref, h_hbm, wg_ref, wu_ref, wd_ref, o_ref,
                   x_buf, x_sem, wg_s, wu_s, wd_s, *, tb):
    b = pl.program_id(0)
    n_blk = pl.num_programs(0)
    slot = lax.rem(b, 2)

    def start_row(tok_ref, dst_slot, r, priority):
        src_row = pl.multiple_of(tok_ref[0, r] * ROW_CHUNKS, ROW_CHUNKS)
        dst_row = pl.multiple_of(r * ROW_CHUNKS, ROW_CHUNKS)
        pltpu.make_async_copy(h_hbm.at[pl.ds(src_row, ROW_CHUNKS), :],
                              x_buf.at[dst_slot, pl.ds(dst_row, ROW_CHUNKS), :],
                              x_sem.at[dst_slot]).start(priority=priority)

    def start_gather(tok_ref, dst_slot, inline):
        if inline:
            for r in range(tb):
                start_row(tok_ref, dst_slot, r, r % 2)
        else:
            def row(r, carry):
                start_row(tok_ref, dst_slot, r, 0)
                return carry
            lax.fori_loop(0, tb, row, 0, unroll=8)

    def wait_gather(dst_slot):
        pltpu.make_async_copy(h_hbm.at[pl.ds(0, tb * ROW_CHUNKS), :], x_buf.at[dst_slot],
                              x_sem.at[dst_slot]).wait()

    def load_block(src_slot):
        return jnp.concatenate([x_buf[src_slot, pl.ds(c, tb, stride=ROW_CHUNKS), :] for c in range(ROW_CHUNKS)],
                               axis=1).astype(BF16)

    @pl.when(b == 0)
    def _first():
        start_gather(tok_cur_ref, 0, inline=False)

    wait_gather(slot)
    prev = be_ref[jnp.maximum(b - 1, 0)]

    @pl.when((b == 0) | (be_ref[b] != prev))
    def _load_expert():
        wg_s[...] = wg_ref[...].astype(BF16)
        wu_s[...] = wu_ref[...].astype(BF16)
        wd_s[...] = wd_ref[...].astype(BF16)

    @pl.when(b < nu_ref[0])
    def _compute():
        start_gather(tok_next_ref, 1 - slot, inline=True)
        x = load_block(slot)
        mid = _silu(_dot(x, wg_s[...])) * _dot(x, wu_s[...])
        o_ref[...] = _dot(mid.astype(BF16), wd_s[...])

    @pl.when(b >= nu_ref[0])
    def _unused():
        start_gather(tok_next_ref, 1 - slot, inline=False)
        o_ref[...] = jnp.zeros_like(o_ref)

    @pl.when(b == n_blk - 1)
    def _drain():
        wait_gather(1 - slot)


def moe_experts(h, row_tok, blk_e, n_used, w_gate, w_up, w_down, layer, tb=MOE_TILE):
    d = D_MODEL
    n_blk = row_tok.shape[0] // tb
    tok3 = row_tok.reshape(n_blk, 1, tb)
    w_in_spec = pl.BlockSpec((None, None, d, D_FF), lambda b, be, nu: (layer, be[b], 0, 0))
    tok_spec = lambda nxt: pl.BlockSpec((None, 1, tb), lambda b, be, nu: (jnp.minimum(b + nxt, n_blk - 1), 0, 0),
                                        memory_space=pltpu.SMEM)
    grid_spec = pltpu.PrefetchScalarGridSpec(
        num_scalar_prefetch=2,
        grid=(n_blk,),
        in_specs=[tok_spec(0), tok_spec(1),
                  pl.BlockSpec(memory_space=pl.ANY),
                  w_in_spec, w_in_spec,
                  pl.BlockSpec((None, None, D_FF, d), lambda b, be, nu: (layer, be[b], 0, 0))],
        out_specs=pl.BlockSpec((tb, d), lambda b, be, nu: (b, 0)),
        scratch_shapes=[pltpu.VMEM((2, tb * ROW_CHUNKS, LANES), F32), pltpu.SemaphoreType.DMA((2,)),
                        pltpu.VMEM((d, D_FF), BF16), pltpu.VMEM((d, D_FF), BF16), pltpu.VMEM((D_FF, d), BF16)],
    )
    return pl.pallas_call(
        functools.partial(_expert_kernel, tb=tb),
        out_shape=jax.ShapeDtypeStruct((n_blk * tb, d), F32),
        grid_spec=grid_spec,
        compiler_params=_params("arbitrary"),
        name="moe_experts",
    )(blk_e, n_used, tok3, tok3, h, w_gate, w_up, w_down)


def _dispatch_plan(experts, tb):
    t, k_sel = experts.shape
    n_asg = t * k_sel
    flat_e = experts.reshape(n_asg)
    onehot = (flat_e[:, None] == jnp.arange(N_EXPERTS, dtype=jnp.int32)[None, :]).astype(jnp.int32)
    csum = jnp.cumsum(onehot, axis=0)
    counts = csum[-1]
    starts = jnp.cumsum(counts) - counts
    padded = (counts + tb - 1) // tb * tb
    pad_end = jnp.cumsum(padded)
    pad_start = pad_end - padded
    dest = jnp.sum(onehot * (csum - 1 + pad_start[None, :]), axis=1)
    n_blk = (n_asg + N_EXPERTS * (tb - 1) + tb - 1) // tb
    blk_row0 = jnp.arange(n_blk, dtype=jnp.int32) * tb
    blk_e = jnp.minimum(jnp.sum((blk_row0[:, None] >= pad_end[None, :]).astype(jnp.int32), axis=1),
                        N_EXPERTS - 1)
    order = jnp.argsort(flat_e).astype(jnp.int32)
    within = (blk_row0 - pad_start[blk_e])[:, None] + jnp.arange(tb, dtype=jnp.int32)[None, :]
    src = jnp.clip(starts[blk_e][:, None] + within, 0, n_asg - 1)
    row_tok = jnp.where(within < counts[blk_e][:, None], order[src.reshape(-1)].reshape(n_blk, tb) // k_sel, 0)
    n_used = (pad_end[-1:] // tb).astype(jnp.int32)
    return row_tok.reshape(-1), dest.reshape(t, k_sel), blk_e.astype(jnp.int32), n_used


def _combine_ple_kernel(x_ref, y0_ref, y1_ref, r_ref, p_ref, g_ref, wg_ref, wp_ref, o_ref):
    route = r_ref[...]
    x = x_ref[...] + route[:, 2:3] * y0_ref[...] + route[:, 3:4] * y1_ref[...]
    h = _rmsnorm(x, g_ref[...]).astype(BF16)
    gate = _sigmoid(_dot(h, wg_ref[...]))
    proj = _dot(p_ref[...].astype(BF16), wp_ref[...])
    o_ref[...] = x + gate * proj


def moe_combine_ple(x, y0, y1, route, row_offset, p, g, wg_bf16, wp_bf16, tm=512):
    t, d = x.shape
    tm = _row_tile(t, tm)
    assert row_offset % tm == 0
    off = row_offset // tm
    pd = p.shape[1]
    row = pl.BlockSpec((tm, d), lambda i: (i, 0))
    row_off = pl.BlockSpec((tm, d), lambda i: (i + off, 0))
    return pl.pallas_call(
        _combine_ple_kernel,
        out_shape=jax.ShapeDtypeStruct((t, d), F32),
        grid=(t // tm,),
        in_specs=[row, row_off, row_off,
                  pl.BlockSpec((tm, ROUTE_LANES), lambda i: (i + off, 0)),
                  pl.BlockSpec((tm, pd), lambda i: (i, 0)),
                  pl.BlockSpec((1, d), lambda i: (0, 0)),
                  pl.BlockSpec((d, d), lambda i: (0, 0)),
                  pl.BlockSpec((pd, d), lambda i: (0, 0))],
        out_specs=row,
        compiler_params=_params("parallel"),
        name="moe_combine_ple",
    )(x, y0, y1, route, p, g.reshape(1, d), wg_bf16, wp_bf16)


def moe_ple_add(xs, ps, layer, norm_g, rg_w, rg_b, re_w, re_b, w_gate, w_up, w_down, ple_g, wg_bf16, wp_bf16,
                tb=MOE_TILE):
    d = xs[0].shape[1]
    w_route = jnp.zeros((d, ROUTE_LANES), F32).at[:, :N_GROUPS].set(rg_w).at[:, N_GROUPS:N_GROUPS + N_EXPERTS].set(re_w)
    b_route = jnp.zeros((1, ROUTE_LANES), F32).at[0, :N_GROUPS].set(rg_b).at[0, N_GROUPS:N_GROUPS + N_EXPERTS].set(re_b)
    h, route, offsets = moe_router(xs, norm_g, w_route, b_route)
    row_tok, dest, blk_e, n_used = _dispatch_plan(route[:, 0:2].astype(jnp.int32), tb)
    yb = moe_experts(h, row_tok, blk_e, n_used, w_gate, w_up, w_down, layer, tb)
    y0, y1 = yb[dest[:, 0]], yb[dest[:, 1]]
    return [moe_combine_ple(x, y0, y1, route, off, p, ple_g, wg_bf16, wp_bf16)
            for x, p, off in zip(xs, ps, offsets)]


def _lambda_init(layer):
    return 0.8 - 0.6 * math.exp(-0.3 * layer)


def _permute_even_in(w_in):
    d = w_in.shape[0]
    lr0 = 2 * GLA_KEY_W + 2 * GLA_VAL_W
    return jnp.concatenate([w_in[:, :lr0], w_in[:, lr0 + GLA_RANK:], w_in[:, lr0:lr0 + GLA_RANK],
                            jnp.zeros((d, LANES - GLA_RANK), w_in.dtype)], axis=1).astype(BF16)


def kernel(x_prompt, x_sample, cache_k, cache_v, state_gla, state_conv, page_table, p_prompt, p_sample, rel_bias, norm_mix, norm_ffn, norm_ple, w_in_ev, w_lr_up, b_lr, gla_norm, conv_w, conv_b, conv_ln_g, conv_ln_b, w_out_ev, w_qkv, q_norm, k_norm, lam_q1, lam_k1, lam_q2, lam_k2, subln, w_out_od, router_g_w, router_g_b, router_e_w, router_e_b, moe_w_gate, moe_w_up, moe_w_down, ple_proj, ple_gate):
    n_bp, seq, d = x_prompt.shape
    n_bs = x_sample.shape[0]
    depth = norm_mix.shape[0]
    n_layers_odd, n_pool, page = cache_k.shape[:3]
    xp = x_prompt.reshape(n_bp * seq, d)
    xs = x_sample.reshape(n_bs, d)
    ck = jnp.transpose(cache_k, (0, 1, 3, 4, 5, 2)).reshape(n_layers_odd, n_pool, d, page)
    cv = cache_v.reshape(n_layers_odd, n_pool, page * DIFF_HEADS, DIFF_DV)
    prompt_bias = prompt_bias_tables(rel_bias)
    step_bias = step_bias_tables(rel_bias, page_table.shape[1] * page)
    step_rows = 16
    kp_l, vp_l, ks_l, vs_l, gp_l, gs_l, cp_l, cs_l = [], [], [], [], [], [], [], []
    for i in range(depth):
        j = i // 2
        if i % 2 == 0:
            w_in = _permute_even_in(w_in_ev[j])
            w_lr_pad = jnp.zeros((LANES, GLA_KEY_W), BF16).at[:GLA_RANK].set(w_lr_up[j].astype(BF16))
            cw_pad = jnp.zeros((CONV_HALO, CONV_CH), F32).at[:CONV_WIDTH].set(conv_w[j])
            w_out = w_out_ev[j].astype(BF16)
            conv_args = (cw_pad, conv_b[j], conv_ln_g[j], conv_ln_b[j])
            zp = norm_proj(xp, norm_mix[i], w_in)
            op, g_p = gla_mix(zp, n_bp, seq, GLA_CHUNK, 512, w_lr_pad, b_lr[j], gla_norm[j], None)
            cp, c_p = conv_mix(zp, n_bp, seq, *conv_args)
            xp = proj_residual([op, cp], w_out, xp)
            zs = norm_proj(xs, norm_mix[i], w_in)
            zs_pad = jnp.zeros((n_bs, step_rows, EVEN_Z_W), F32).at[:, 0].set(zs).reshape(n_bs * step_rows, EVEN_Z_W)
            os_pad, g_s = gla_mix(zs_pad, n_bs, step_rows, step_rows, 8 * step_rows, w_lr_pad, b_lr[j], gla_norm[j],
                                  state_gla[j].reshape(n_bs, GLA_KEY_W, GLA_DV), valid_rows=1)
            os_ = os_pad.reshape(n_bs, step_rows, GLA_VAL_W)[:, 0]
            cs, c_s = conv_step(zs, state_conv[j], *conv_args)
            xs = proj_residual([os_, cs], w_out, xs)
            gp_l.append(g_p.reshape(n_bp, GLA_HEADS, GLA_DK, GLA_DV))
            gs_l.append(g_s.reshape(n_bs, GLA_HEADS, GLA_DK, GLA_DV))
            cp_l.append(c_p)
            cs_l.append(c_s)
        else:
            lam0 = _lambda_init(i)
            lamv = jnp.stack([lam_q1[j], lam_k1[j], lam_q2[j], lam_k2[j]]).astype(F32)
            wq = w_qkv[j].astype(BF16)
            w_out = w_out_od[j].astype(BF16)
            qp, kp, vp, kpb, vpb = qkv_proj(xp, norm_mix[i], wq, q_norm[j], k_norm[j])
            ap = diff_attention(qp, kpb, vpb, n_bp, seq, prompt_bias, lamv, subln[j], lam0)
            xp = proj_residual([ap], w_out, xp)
            qs, ks, vs, _, _ = qkv_proj(xs, norm_mix[i], wq, q_norm[j], k_norm[j])
            as_ = diff_attention_step(qs, ks, vs, ck, cv, j, page_table, step_bias, lamv, subln[j], lam0)
            xs = proj_residual([as_], w_out, xs)
            kp_l.append(kp.reshape(n_bp, seq, DIFF_HEADS, 2, DIFF_DH))
            vp_l.append(vp.reshape(n_bp, seq, DIFF_HEADS, DIFF_DV))
            ks_l.append(ks.reshape(n_bs, 1, DIFF_HEADS, 2, DIFF_DH))
            vs_l.append(vs.reshape(n_bs, 1, DIFF_HEADS, DIFF_DV))
        tail_args = (i, norm_ffn[i], router_g_w[i], router_g_b[i], router_e_w[i], router_e_b[i],
                     moe_w_gate, moe_w_up, moe_w_down, norm_ple[i], ple_gate[i].astype(BF16), ple_proj[i].astype(BF16))
        xp, xs = moe_ple_add([xp, xs], [p_prompt[i].reshape(n_bp * seq, PLE_DIM), p_sample[i].reshape(n_bs, PLE_DIM)],
                             *tail_args)
    return (xp.reshape(n_bp, seq, d), xs.reshape(n_bs, 1, d),
            jnp.stack(kp_l), jnp.stack(vp_l), jnp.stack(ks_l), jnp.stack(vs_l),
            jnp.stack(gp_l), jnp.stack(gs_l), jnp.stack(cp_l), jnp.stack(cs_l))
```

```python
import functools
import math

import numpy as np
import jax
import jax.numpy as jnp
from jax import lax
from jax.experimental import pallas as pl
from jax.experimental.pallas import tpu as pltpu

F32 = jnp.float32
BF16 = jnp.bfloat16

D_MODEL = 1024
NORM_EPS = 1e-6
GLA_HEADS = 4
GLA_DK = 64
GLA_DV = 128
GLA_KEY_W = GLA_HEADS * GLA_DK
GLA_VAL_W = GLA_HEADS * GLA_DV
GLA_RANK = 16
GLA_TAU = 16.0
GLA_CHUNK = 64
CONV_CH = 512
CONV_WIDTH = 31
CONV_HALO = 32
DIFF_HEADS = 8
DIFF_DH = 64
DIFF_DV = 128
REL_BUCKETS = 32
REL_MAX_DIST = 128
ATT_TILE = 512
MASK_VALUE = -1e30
N_GROUPS = 4
EXPERTS_PER_GROUP = 8
N_EXPERTS = N_GROUPS * EXPERTS_PER_GROUP
D_FF = 512
MOE_TILE = 256
ROUTE_LANES = 128
PLE_DIM = 256

LANES = 128
VMEM_LIMIT_BYTES = 56 * 1024 * 1024
EVEN_Z_W = 2 * GLA_KEY_W + 2 * GLA_VAL_W + 2 * CONV_CH + LANES
EVEN_LR_BLOCK = (2 * GLA_KEY_W + 2 * GLA_VAL_W + 2 * CONV_CH) // LANES


def _params(*sem):
    return pltpu.CompilerParams(dimension_semantics=sem, vmem_limit_bytes=VMEM_LIMIT_BYTES)


def _dot(a, b):
    return jnp.dot(a, b, preferred_element_type=F32)


def _dot_nt(a, b):
    return lax.dot_general(a, b, (((1,), (1,)), ((), ())), preferred_element_type=F32)


def _dot_tn(a, b):
    return lax.dot_general(a, b, (((0,), (0,)), ((), ())), preferred_element_type=F32)


def _split_bf16(x):
    hi = x.astype(BF16)
    lo = (x - hi.astype(F32)).astype(BF16)
    return hi, lo


def _rmsnorm(x, g):
    ms = jnp.mean(x * x, axis=-1, keepdims=True)
    return x * lax.rsqrt(ms + NORM_EPS) * g


def _sigmoid(x):
    return 1.0 / (1.0 + jnp.exp(-x))


def _silu(x):
    return x * _sigmoid(x)


def _row_tile(n_rows, want):
    t = min(want, n_rows)
    assert n_rows % t == 0, (n_rows, t)
    return t


def _norm_proj_kernel(x_ref, g_ref, w_ref, o_ref):
    h = _rmsnorm(x_ref[...], g_ref[...]).astype(BF16)
    o_ref[...] = _dot(h, w_ref[...])


def norm_proj(x, g, w_bf16, tm=256):
    t, d = x.shape
    n = w_bf16.shape[1]
    tm = _row_tile(t, tm)
    return pl.pallas_call(
        _norm_proj_kernel,
        out_shape=jax.ShapeDtypeStruct((t, n), F32),
        grid=(t // tm,),
        in_specs=[pl.BlockSpec((tm, d), lambda i: (i, 0)),
                  pl.BlockSpec((1, d), lambda i: (0, 0)),
                  pl.BlockSpec((d, n), lambda i: (0, 0))],
        out_specs=pl.BlockSpec((tm, n), lambda i: (i, 0)),
        compiler_params=_params("parallel"),
        name="norm_proj",
    )(x, g.reshape(1, d), w_bf16)


def _proj_residual_kernel(*refs, n_in):
    a_refs = refs[:n_in]
    w_ref, x_ref, o_ref = refs[n_in:]
    acc = x_ref[...]
    row = 0
    for a_ref in a_refs:
        k = a_ref.shape[1]
        acc = acc + _dot(a_ref[...].astype(BF16), w_ref[row:row + k, :])
        row += k
    o_ref[...] = acc


def proj_residual(acts, w_bf16, x, tm=512):
    t, d = x.shape
    tm = _row_tile(t, tm)
    n_in = len(acts)
    in_specs = [pl.BlockSpec((tm, a.shape[1]), lambda i: (i, 0)) for a in acts]
    in_specs += [pl.BlockSpec(w_bf16.shape, lambda i: (0, 0)),
                 pl.BlockSpec((tm, d), lambda i: (i, 0))]
    return pl.pallas_call(
        functools.partial(_proj_residual_kernel, n_in=n_in),
        out_shape=jax.ShapeDtypeStruct((t, d), F32),
        grid=(t // tm,),
        in_specs=in_specs,
        out_specs=pl.BlockSpec((tm, d), lambda i: (i, 0)),
        compiler_params=_params("parallel"),
        name="proj_residual",
    )(*acts, w_bf16, x)


def _ple_kernel(x_ref, p_ref, g_ref, wg_ref, wp_ref, o_ref):
    x = x_ref[...]
    h = _rmsnorm(x, g_ref[...]).astype(BF16)
    gate = _sigmoid(_dot(h, wg_ref[...]))
    proj = _dot(p_ref[...].astype(BF16), wp_ref[...])
    o_ref[...] = x + gate * proj


def ple_add(x, p, g, wg_bf16, wp_bf16, tm=512):
    t, d = x.shape
    tm = _row_tile(t, tm)
    pd = p.shape[1]
    return pl.pallas_call(
        _ple_kernel,
        out_shape=jax.ShapeDtypeStruct((t, d), F32),
        grid=(t // tm,),
        in_specs=[pl.BlockSpec((tm, d), lambda i: (i, 0)),
                  pl.BlockSpec((tm, pd), lambda i: (i, 0)),
                  pl.BlockSpec((1, d), lambda i: (0, 0)),
                  pl.BlockSpec((d, d), lambda i: (0, 0)),
                  pl.BlockSpec((pd, d), lambda i: (0, 0))],
        out_specs=pl.BlockSpec((tm, d), lambda i: (i, 0)),
        compiler_params=_params("parallel"),
        name="ple_add",
    )(x, p, g.reshape(1, d), wg_bf16, wp_bf16)


def _gla_constants(chunk):
    c = chunk
    t = np.arange(c)
    tri = (t[None, :] <= t[:, None]).astype(np.float32)
    causal4 = np.tile(tri, (GLA_HEADS, 1))
    lane_head = np.arange(GLA_KEY_W) // GLA_DK
    row_head = np.repeat(np.arange(GLA_HEADS), c)
    headmask4 = (row_head[:, None] == lane_head[None, :]).astype(np.float32)
    blockdiag = (np.arange(GLA_KEY_W)[:, None] // GLA_DK
                 == np.arange(GLA_VAL_W)[None, :] // GLA_DV).astype(np.float32)
    return (jnp.asarray(tri, BF16), jnp.asarray(causal4), jnp.asarray(headmask4), jnp.asarray(blockdiag))


def _gla_kernel(*refs, chunk, n_chunks, has_state, valid_rows, chunk_is_sequence):
    if has_state:
        (qk_ref, v_ref, r_ref, lr_ref, wlr_ref, blr_ref, gg_ref, tri_ref, causal_ref, hmask_ref, bd_ref,
         s0_ref, o_ref, sout_ref, s_scr) = refs
    else:
        (qk_ref, v_ref, r_ref, lr_ref, wlr_ref, blr_ref, gg_ref, tri_ref, causal_ref, hmask_ref, bd_ref,
         o_ref, sout_ref, s_scr) = refs
    c = chunk
    i = pl.program_id(1)
    bd = bd_ref[...]

    def to_block_diag(s):
        return jnp.concatenate([s] * GLA_HEADS, axis=1) * bd

    def store_state(s_bd, slot):
        for h in range(GLA_HEADS):
            sout_ref[slot, h * GLA_DK:(h + 1) * GLA_DK, :] = s_bd[h * GLA_DK:(h + 1) * GLA_DK,
                                                                  h * GLA_DV:(h + 1) * GLA_DV]

    if not chunk_is_sequence:
        @pl.when(i == 0)
        def _init():
            if has_state:
                s_scr[...] = to_block_diag(s0_ref[0])
            else:
                s_scr[...] = jnp.zeros_like(s_scr)

    tri = tri_ref[...]
    ones = jnp.ones((c, LANES), BF16)

    def chunk_step(ci, carry):
        rows = pl.ds(pl.multiple_of(ci * c, c), c)
        qk = qk_ref[rows, :]
        q = qk[:, :GLA_KEY_W] * (GLA_DK ** -0.5)
        k = qk[:, GLA_KEY_W:]
        vb = v_ref[rows, :].astype(BF16)
        x = _dot(lr_ref[rows, :].astype(BF16), wlr_ref[...]) + blr_ref[...]
        g = (jnp.minimum(x, 0.0) - jnp.log1p(jnp.exp(-jnp.abs(x)))) * (1.0 / GLA_TAU)
        if valid_rows < c:
            g = jnp.where(lax.broadcasted_iota(jnp.int32, g.shape, 0) < valid_rows, g, 0.0)
        g_hi, g_lo = _split_bf16(g)
        b = _dot(tri, g_hi) + _dot(tri, g_lo)
        dcol = _dot_tn(g_hi, ones) + _dot_tn(g_lo, ones)
        bend = b[c - 1:c, :]
        half = 0.5 * bend
        qs = q * jnp.exp(b - half)
        ks = k * jnp.exp(half - b)
        qt = q * jnp.exp(b)
        kh = k * jnp.exp(bend - b)
        qstack = (jnp.concatenate([qs] * GLA_HEADS, axis=0) * hmask_ref[...]).astype(BF16)
        scores = (_dot_nt(qstack, ks.astype(BF16)) * causal_ref[...]).astype(BF16)
        s_bd = to_block_diag(s0_ref[ci]) if chunk_is_sequence else s_scr[...]
        o_inter = _dot(qt.astype(BF16), s_bd.astype(BF16))
        gg = gg_ref[...]
        for h in range(GLA_HEADS):
            cols = slice(h * GLA_DV, (h + 1) * GLA_DV)
            o_h = _dot(scores[h * c:(h + 1) * c, :], vb[:, cols]) + o_inter[:, cols]
            y = _rmsnorm(o_h, gg) * _silu(r_ref[rows, cols])
            o_ref[rows, cols] = y.astype(o_ref.dtype)
        decay = jnp.concatenate([jnp.exp(dcol)] * GLA_HEADS, axis=1)
        s_new = s_bd * decay + _dot_tn(kh.astype(BF16), vb) * bd
        if chunk_is_sequence:
            store_state(s_new, ci)
        else:
            s_scr[...] = s_new
        return carry

    lax.fori_loop(0, n_chunks, chunk_step, 0, unroll=2 if n_chunks % 2 == 0 else 1)

    if not chunk_is_sequence:
        @pl.when(i == pl.num_programs(1) - 1)
        def _final():
            store_state(s_scr[...], 0)


def gla_mix(z, n_batch, seq, chunk, tile, w_lr_pad, b_lr, gla_g, state0, valid_rows=None):
    chunk_is_sequence = seq == chunk and tile > seq
    if chunk_is_sequence:
        assert state0 is not None and n_batch % (tile // seq) == 0
        seqs = tile // seq
        n_groups, nt = n_batch // seqs, 1
    else:
        assert seq % tile == 0 and tile % chunk == 0
        seqs, n_groups, nt = 1, n_batch, seq // tile
    valid_rows = chunk if valid_rows is None else valid_rows
    assert valid_rows == chunk or seq == chunk
    has_state = state0 is not None
    consts = _gla_constants(chunk)
    row = lambda b, i: b * nt + i
    in_specs = [pl.BlockSpec((tile, 2 * GLA_KEY_W), lambda b, i: (row(b, i), 0)),
                pl.BlockSpec((tile, GLA_VAL_W), lambda b, i: (row(b, i), 1)),
                pl.BlockSpec((tile, GLA_VAL_W), lambda b, i: (row(b, i), 2)),
                pl.BlockSpec((tile, LANES), lambda b, i: (row(b, i), EVEN_LR_BLOCK)),
                pl.BlockSpec(w_lr_pad.shape, lambda b, i: (0, 0)),
                pl.BlockSpec((1, GLA_KEY_W), lambda b, i: (0, 0)),
                pl.BlockSpec((1, GLA_DV), lambda b, i: (0, 0))]
    in_specs += [pl.BlockSpec(a.shape, lambda b, i: (0, 0)) for a in consts]
    args = [z, z, z, z, w_lr_pad, b_lr.reshape(1, GLA_KEY_W), gla_g.reshape(1, GLA_DV), *consts]
    state_spec = pl.BlockSpec((seqs, GLA_KEY_W, GLA_DV), lambda b, i: (b, 0, 0))
    if has_state:
        in_specs.append(state_spec)
        args.append(state0)
    return pl.pallas_call(
        functools.partial(_gla_kernel, chunk=chunk, n_chunks=tile // chunk, has_state=has_state,
                          valid_rows=valid_rows, chunk_is_sequence=chunk_is_sequence),
        out_shape=(jax.ShapeDtypeStruct((n_batch * seq, GLA_VAL_W), BF16),
                   jax.ShapeDtypeStruct((n_batch, GLA_KEY_W, GLA_DV), F32)),
        grid=(n_groups, nt),
        in_specs=in_specs,
        out_specs=(pl.BlockSpec((tile, GLA_VAL_W), lambda b, i: (row(b, i), 0)), state_spec),
        scratch_shapes=[pltpu.VMEM((GLA_KEY_W, GLA_VAL_W), F32)],
        compiler_params=_params("parallel", "arbitrary"),
        name="gla_mix",
    )(*args)


def _layernorm_silu(c, g, b):
    mu = jnp.mean(c, axis=-1, keepdims=True)
    cc = c - mu
    var = jnp.mean(cc * cc, axis=-1, keepdims=True)
    return _silu(cc * lax.rsqrt(var + NORM_EPS) * g + b)


def _conv_kernel(a_ref, gt_ref, cw_ref, cb_ref, lg_ref, lb_ref, o_ref, st_ref, ext, shifted, *, tile, sub):
    i = pl.program_id(1)
    n_state = CONV_WIDTH - 1
    sublanes = 8
    span = tile + CONV_HALO - sublanes

    @pl.when(i == 0)
    def _zero_halo():
        ext[0:CONV_HALO, :] = jnp.zeros((CONV_HALO, CONV_CH), F32)

    @pl.when(i > 0)
    def _carry_halo():
        ext[0:CONV_HALO, :] = ext[tile:tile + CONV_HALO, :]

    ext[CONV_HALO:CONV_HALO + tile, :] = a_ref[...] * _sigmoid(gt_ref[...])
    for ph in range(1, sublanes):
        shifted[ph, 0:span, :] = ext[ph:ph + span, :]
    lg = lg_ref[...]
    lb = lb_ref[...]
    for s in range(tile // sub):
        acc = jnp.broadcast_to(cb_ref[...], (sub, CONV_CH))
        for w in range(CONV_WIDTH):
            off = (CONV_HALO - n_state) + w
            ph, base = off % sublanes, s * sub + off - off % sublanes
            rows = ext[base:base + sub, :] if ph == 0 else shifted[ph, base:base + sub, :]
            acc = acc + rows * cw_ref[w:w + 1, :]
        o_ref[s * sub:(s + 1) * sub, :] = _layernorm_silu(acc, lg, lb).astype(o_ref.dtype)

    @pl.when(i == pl.num_programs(1) - 1)
    def _state():
        st_ref[0] = ext[tile + CONV_HALO - n_state:tile + CONV_HALO, :]


def conv_mix(z, n_batch, seq, conv_w_pad, conv_b, ln_g, ln_b, tile=256, sub=32):
    nt = seq // tile
    assert seq % tile == 0 and tile % sub == 0 and tile >= CONV_HALO
    row = lambda b, i: b * nt + i
    vec = lambda a: a.reshape(1, CONV_CH)
    return pl.pallas_call(
        functools.partial(_conv_kernel, tile=tile, sub=sub),
        out_shape=(jax.ShapeDtypeStruct((n_batch * seq, CONV_CH), BF16),
                   jax.ShapeDtypeStruct((n_batch, CONV_WIDTH - 1, CONV_CH), F32)),
        grid=(n_batch, nt),
        in_specs=[pl.BlockSpec((tile, CONV_CH), lambda b, i: (row(b, i), 3)),
                  pl.BlockSpec((tile, CONV_CH), lambda b, i: (row(b, i), 4)),
                  pl.BlockSpec(conv_w_pad.shape, lambda b, i: (0, 0)),
                  pl.BlockSpec((1, CONV_CH), lambda b, i: (0, 0)),
                  pl.BlockSpec((1, CONV_CH), lambda b, i: (0, 0)),
                  pl.BlockSpec((1, CONV_CH), lambda b, i: (0, 0))],
        out_specs=(pl.BlockSpec((tile, CONV_CH), lambda b, i: (row(b, i), 0)),
                   pl.BlockSpec((1, CONV_WIDTH - 1, CONV_CH), lambda b, i: (b, 0, 0))),
        scratch_shapes=[pltpu.VMEM((tile + CONV_HALO, CONV_CH), F32),
                        pltpu.VMEM((8, tile + CONV_HALO, CONV_CH), F32)],
        compiler_params=_params("parallel", "arbitrary"),
        name="conv_mix",
    )(z, z, conv_w_pad, vec(conv_b), vec(ln_g), vec(ln_b))


def _conv_step_kernel(a_ref, gt_ref, st_ref, cw_ref, cb_ref, lg_ref, lb_ref, o_ref, sto_ref):
    n_state = CONV_WIDTH - 1
    u = a_ref[...] * _sigmoid(gt_ref[...])
    acc = cb_ref[...] + u * cw_ref[n_state:n_state + 1, :]
    for w in range(n_state):
        acc = acc + st_ref[:, w * CONV_CH:(w + 1) * CONV_CH] * cw_ref[w:w + 1, :]
    o_ref[...] = _layernorm_silu(acc, lg_ref[...], lb_ref[...]).astype(o_ref.dtype)
    sto_ref[:, 0:(n_state - 1) * CONV_CH] = st_ref[:, CONV_CH:n_state * CONV_CH]
    sto_ref[:, (n_state - 1) * CONV_CH:] = u


def conv_step(z, state, conv_w_pad, conv_b, ln_g, ln_b, tile=32):
    nb = z.shape[0]
    n_state = CONV_WIDTH - 1
    tile = _row_tile(nb, tile)
    vec = lambda a: a.reshape(1, CONV_CH)
    st2 = state.reshape(nb, n_state * CONV_CH)
    out, st_new = pl.pallas_call(
        _conv_step_kernel,
        out_shape=(jax.ShapeDtypeStruct((nb, CONV_CH), BF16),
                   jax.ShapeDtypeStruct((nb, n_state * CONV_CH), F32)),
        grid=(nb // tile,),
        in_specs=[pl.BlockSpec((tile, CONV_CH), lambda i: (i, 3)),
                  pl.BlockSpec((tile, CONV_CH), lambda i: (i, 4)),
                  pl.BlockSpec((tile, n_state * CONV_CH), lambda i: (i, 0)),
                  pl.BlockSpec(conv_w_pad.shape, lambda i: (0, 0)),
                  pl.BlockSpec((1, CONV_CH), lambda i: (0, 0)),
                  pl.BlockSpec((1, CONV_CH), lambda i: (0, 0)),
                  pl.BlockSpec((1, CONV_CH), lambda i: (0, 0))],
        out_specs=(pl.BlockSpec((tile, CONV_CH), lambda i: (i, 0)),
                   pl.BlockSpec((tile, n_state * CONV_CH), lambda i: (i, 0))),
        compiler_params=_params("parallel"),
        name="conv_step",
    )(z, z, st2, conv_w_pad, vec(conv_b), vec(ln_g), vec(ln_b))
    return out, st_new.reshape(nb, n_state, CONV_CH)


def _group_constants():
    grp = np.zeros((D_MODEL, LANES), np.float32)
    grp[np.arange(D_MODEL), np.arange(D_MODEL) // DIFF_DH] = 1.0
    return jnp.asarray(grp, BF16), jnp.asarray(grp.T.copy(), BF16)


def _qkv_kernel(x_ref, g_ref, w_ref, grp_ref, grpt_ref, qg_ref, kg_ref, q_ref, k_ref, v_ref, kb_ref, vb_ref):
    h = _rmsnorm(x_ref[...], g_ref[...]).astype(BF16)
    z = _dot(h, w_ref[...])
    grp = grp_ref[...]
    grpt = grpt_ref[...]

    def map_norm(y, gain):
        hi, lo = _split_bf16(y * y)
        ms = (_dot(hi, grp) + _dot(lo, grp)) * (1.0 / DIFF_DH)
        rh, rl = _split_bf16(lax.rsqrt(ms + NORM_EPS))
        return y * (_dot(rh, grpt) + _dot(rl, grpt)) * gain

    qn = map_norm(z[:, :D_MODEL], qg_ref[...])
    kn = map_norm(z[:, D_MODEL:2 * D_MODEL], kg_ref[...])
    v = z[:, 2 * D_MODEL:]
    q_ref[...] = (qn * (DIFF_DH ** -0.5)).astype(BF16)
    k_ref[...] = kn
    v_ref[...] = v
    kb_ref[...] = kn.astype(BF16)
    ones = jnp.ones((v.shape[0], DIFF_DV), BF16)
    for h in range(DIFF_HEADS):
        vb_ref[:, 2 * h * DIFF_DV:(2 * h + 1) * DIFF_DV] = v[:, h * DIFF_DV:(h + 1) * DIFF_DV].astype(BF16)
        vb_ref[:, (2 * h + 1) * DIFF_DV:(2 * h + 2) * DIFF_DV] = ones


def qkv_proj(x, g, w_bf16, q_gain, k_gain, tm=256):
    t, d = x.shape
    tm = _row_tile(t, tm)
    grp, grpt = _group_constants()
    tile_gain = lambda a: jnp.tile(a, D_MODEL // DIFF_DH).reshape(1, D_MODEL)
    row = pl.BlockSpec((tm, d), lambda i: (i, 0))
    full = lambda a: pl.BlockSpec(a.shape, lambda i: (0, 0))
    qg, kg = tile_gain(q_gain), tile_gain(k_gain)
    return pl.pallas_call(
        _qkv_kernel,
        out_shape=(jax.ShapeDtypeStruct((t, d), BF16), jax.ShapeDtypeStruct((t, d), F32),
                   jax.ShapeDtypeStruct((t, d), F32), jax.ShapeDtypeStruct((t, d), BF16),
                   jax.ShapeDtypeStruct((t, 2 * d), BF16)),
        grid=(t // tm,),
        in_specs=[row, pl.BlockSpec((1, d), lambda i: (0, 0)), full(w_bf16), full(grp), full(grpt),
                  full(qg), full(kg)],
        out_specs=(row, row, row, row, pl.BlockSpec((tm, 2 * d), lambda i: (i, 0))),
        compiler_params=_params("parallel"),
        name="qkv_proj",
    )(x, g.reshape(1, d), w_bf16, grp, grpt, qg, kg)


def _rel_bucket_np(n):
    max_exact = REL_BUCKETS // 2
    nf = np.maximum(n, 1).astype(np.float64)
    large = max_exact + (np.log(nf / max_exact) / math.log(REL_MAX_DIST / max_exact)
                         * (REL_BUCKETS - max_exact)).astype(np.int32)
    return np.where(n < max_exact, np.maximum(n, 0), np.minimum(large, REL_BUCKETS - 1))


def _diff_lambda(lamv, lam0):
    s1 = jnp.sum(lamv[0:1, :] * lamv[1:2, :], axis=-1, keepdims=True)
    s2 = jnp.sum(lamv[2:3, :] * lamv[3:4, :], axis=-1, keepdims=True)
    return jnp.exp(s1) - jnp.exp(s2) + lam0


def _attn_kernel(q_ref, k_ref, v_ref, bdiag_ref, bsub_ref, bfar_ref, lamv_ref, sub_ref, o_ref,
                 m_scr, acc_scr, *, tile, lam0):
    qi = pl.program_id(2)
    q = q_ref[...]
    lane = lax.broadcasted_iota(jnp.int32, q.shape, 1)
    zero = jnp.zeros_like(q)
    q_maps = (jnp.where(lane < DIFF_DH, q, zero), jnp.where(lane >= DIFF_DH, q, zero))
    reps = tile // LANES

    m_scr[...] = jnp.full(m_scr.shape, -jnp.inf, F32)
    acc_scr[...] = jnp.zeros(acc_scr.shape, F32)

    def flash_step(kt, bias):
        rows = pl.ds(pl.multiple_of(kt * tile, tile), tile)
        k = k_ref[rows, :]
        v = v_ref[rows, :]
        for m in range(2):
            s = _dot_nt(q_maps[m], k) + bias
            m_prev = m_scr[m]
            m_new = jnp.maximum(m_prev, jnp.max(s, axis=1, keepdims=True))
            alpha = jnp.exp(m_prev - m_new)
            p = jnp.exp(s - jnp.concatenate([m_new] * reps, axis=1))
            acc_scr[m] = jnp.concatenate([alpha, alpha], axis=1) * acc_scr[m] + _dot(p.astype(BF16), v)
            m_scr[m] = m_new

    n_far = jnp.maximum(qi - 1, 0)

    def far_pair(i, carry):
        flash_step(2 * i, bfar_ref[0])
        flash_step(2 * i + 1, bfar_ref[0])
        return carry

    lax.fori_loop(0, n_far // 2, far_pair, 0)

    @pl.when(lax.rem(n_far, 2) == 1)
    def _far_tail():
        flash_step(n_far - 1, bfar_ref[0])

    @pl.when(qi >= 1)
    def _sub_and_diag():
        flash_step(qi - 1, bsub_ref[0])
        flash_step(qi, bdiag_ref[0])

    @pl.when(qi == 0)
    def _diag_only():
        flash_step(qi, bdiag_ref[0])

    lam = _diff_lambda(lamv_ref[...], lam0)
    a0 = acc_scr[0]
    a1 = acc_scr[1]
    o = a0[:, :DIFF_DV] / a0[:, DIFF_DV:] - lam * (a1[:, :DIFF_DV] / a1[:, DIFF_DV:])
    o_ref[...] = (_rmsnorm(o, sub_ref[...]) * (1.0 - lam0)).astype(o_ref.dtype)


def _bucket_lookup(rel_bias, n):
    onehot = np.zeros((n.size, REL_BUCKETS), np.float32)
    onehot[np.arange(n.size), _rel_bucket_np(n).reshape(-1)] = 1.0
    out = jnp.dot(jnp.asarray(onehot), rel_bias.astype(F32), precision=lax.Precision.HIGHEST)
    return out.reshape(n.shape + (rel_bias.shape[1],))


def prompt_bias_tables(rel_bias, tile=ATT_TILE):
    r = np.arange(tile)
    n_diag = r[:, None] - r[None, :]
    bdiag = jnp.where(jnp.asarray(n_diag >= 0)[None], jnp.moveaxis(_bucket_lookup(rel_bias, n_diag), -1, 0),
                      MASK_VALUE)
    bsub = jnp.moveaxis(_bucket_lookup(rel_bias, n_diag + tile), -1, 0)
    assert tile + 1 >= REL_MAX_DIST
    bfar = jnp.broadcast_to(rel_bias[REL_BUCKETS - 1][:, None, None], (DIFF_HEADS, 1, tile))
    return bdiag.astype(F32), bsub.astype(F32), bfar.astype(F32)


def diff_attention(q, kb, vb1, n_batch, seq, bias_tables, lamv, subln, lam0, tile=ATT_TILE):
    nq = seq // tile
    assert seq % tile == 0
    bdiag, bsub, bfar = bias_tables
    head_tile = lambda b, h, i: (h, 0, 0)
    return pl.pallas_call(
        functools.partial(_attn_kernel, tile=tile, lam0=lam0),
        out_shape=jax.ShapeDtypeStruct((n_batch * seq, D_MODEL), BF16),
        grid=(n_batch, DIFF_HEADS, nq),
        in_specs=[pl.BlockSpec((tile, DIFF_DV), lambda b, h, i: (b * nq + i, h)),
                  pl.BlockSpec((seq, DIFF_DV), lambda b, h, i: (b, h)),
                  pl.BlockSpec((seq, 2 * DIFF_DV), lambda b, h, i: (b, h)),
                  pl.BlockSpec((1, tile, tile), head_tile),
                  pl.BlockSpec((1, tile, tile), head_tile),
                  pl.BlockSpec((1, 1, tile), head_tile),
                  pl.BlockSpec((4, DIFF_DH), lambda b, h, i: (0, 0)),
                  pl.BlockSpec((1, DIFF_DV), lambda b, h, i: (0, 0))],
        out_specs=pl.BlockSpec((tile, DIFF_DV), lambda b, h, i: (b * nq + i, h)),
        scratch_shapes=[pltpu.VMEM((2, tile, LANES), F32), pltpu.VMEM((2, tile, 2 * DIFF_DV), F32)],
        compiler_params=_params("parallel", "parallel", "arbitrary"),
        name="diff_attention",
    )(q, kb, vb1, bdiag, bsub, bfar, lamv, subln.reshape(1, DIFF_DV))


N_MAPS = 2 * DIFF_HEADS


def step_bias_tables(rel_bias, n_past):
    n = n_past - np.arange(n_past)
    past = _bucket_lookup(rel_bias, n).T
    new = jnp.broadcast_to(_bucket_lookup(rel_bias, np.zeros((1,), np.int64)).T, (DIFF_HEADS, LANES))
    return jnp.concatenate([past, past], axis=0), jnp.concatenate([new, new], axis=0)


def _step_constants(page):
    j = np.arange(N_MAPS)
    c = np.arange(D_MODEL)
    col_head, col_map = c // DIFF_DV, (c % DIFF_DV) // DIFF_DH
    sel = ((j[:, None] % DIFF_HEADS == col_head[None, :])
           & (j[:, None] // DIFF_HEADS == col_map[None, :])).astype(np.float32)
    r = np.arange(page * DIFF_HEADS)
    tok_of_row = (r[:, None] // DIFF_HEADS == np.arange(page)[None, :]).astype(np.float32)
    return jnp.asarray(sel), jnp.asarray(tok_of_row)


def _attn_step_kernel(pt_ref, q_ref, kn_ref, vn_ref, *refs, n_pages, page, lam0):
    kt_refs = refs[:n_pages]
    v_refs = refs[n_pages:2 * n_pages]
    (bias_ref, nbias_ref, sel_ref, tok_ref, lamv_ref, sub_ref, o_ref, s_scr) = refs[2 * n_pages:]
    sel = sel_ref[...]
    q_sel = sel * q_ref[0].astype(F32)
    q_bf = q_sel.astype(BF16)
    for p in range(n_pages):
        cols = slice(p * page, (p + 1) * page)
        s_scr[:, cols] = _dot(q_bf, kt_refs[p][...].astype(BF16)) + bias_ref[:, cols]
    s_new = jnp.sum(q_sel * kn_ref[0], axis=1, keepdims=True) + nbias_ref[:, 0:1]
    s = s_scr[...]
    m = jnp.maximum(jnp.max(s, axis=1, keepdims=True), s_new)
    p_all = jnp.exp(s - m)
    p_new = jnp.exp(s_new - m)
    l = jnp.sum(p_all, axis=1, keepdims=True) + p_new
    lam = _diff_lambda(lamv_ref[...], lam0)
    row = lax.broadcasted_iota(jnp.int32, (N_MAPS, 1), 0)
    coef = jnp.where(row < DIFF_HEADS, 1.0 / l, -lam / l)
    w_all = p_all * coef
    w_heads = w_all[:DIFF_HEADS, :] + w_all[DIFF_HEADS:, :]
    w_new = (p_new * coef)[:DIFF_HEADS, :] + (p_new * coef)[DIFF_HEADS:, :]
    tok_of_row = tok_ref[...]
    acc = w_new * vn_ref[0]
    for p in range(n_pages):
        w_p = w_heads[:, p * page:(p + 1) * page]
        w_rows = jnp.broadcast_to(w_p[None], (page, DIFF_HEADS, page)).reshape(page * DIFF_HEADS, page)
        w_col = jnp.sum(w_rows * tok_of_row, axis=1, keepdims=True)
        acc = acc + jnp.sum((w_col * v_refs[p][...]).reshape(page, DIFF_HEADS, DIFF_DV), axis=0)
    o_ref[0] = _rmsnorm(acc, sub_ref[...]) * (1.0 - lam0)


def diff_attention_step(q, k_new, v_new, cache_kt, cache_v2, layer, page_table, bias_tables, lamv, subln, lam0):
    nb = q.shape[0]
    n_pages = page_table.shape[1]
    page = cache_kt.shape[3]
    n_past = n_pages * page
    past_bias, new_bias = bias_tables
    sel, tok_of_row = _step_constants(page)
    tok_spec = pl.BlockSpec((1, 1, D_MODEL), lambda b, pt: (b, 0, 0))
    head_spec = pl.BlockSpec((1, DIFF_HEADS, DIFF_DV), lambda b, pt: (b, 0, 0))
    kt_spec = lambda p: pl.BlockSpec((None, None, D_MODEL, page),
                                     lambda b, pt: (layer, pt[b * n_pages + p], 0, 0))
    v_spec = lambda p: pl.BlockSpec((None, None, page * DIFF_HEADS, DIFF_DV),
                                    lambda b, pt: (layer, pt[b * n_pages + p], 0, 0))
    full = lambda a: pl.BlockSpec(a.shape, lambda b, pt: (0,) * a.ndim)
    sub2 = subln.reshape(1, DIFF_DV)
    grid_spec = pltpu.PrefetchScalarGridSpec(
        num_scalar_prefetch=1,
        grid=(nb,),
        in_specs=([tok_spec, tok_spec, head_spec]
                  + [kt_spec(p) for p in range(n_pages)] + [v_spec(p) for p in range(n_pages)]
                  + [full(past_bias), full(new_bias), full(sel), full(tok_of_row), full(lamv), full(sub2)]),
        out_specs=head_spec,
        scratch_shapes=[pltpu.VMEM((N_MAPS, n_past), F32)],
    )
    out = pl.pallas_call(
        functools.partial(_attn_step_kernel, n_pages=n_pages, page=page, lam0=lam0),
        out_shape=jax.ShapeDtypeStruct((nb, DIFF_HEADS, DIFF_DV), F32),
        grid_spec=grid_spec,
        compiler_params=_params("parallel"),
        name="diff_attention_step",
    )(page_table.reshape(-1), q.reshape(nb, 1, D_MODEL), k_new.reshape(nb, 1, D_MODEL),
      v_new.reshape(nb, DIFF_HEADS, DIFF_DV),
      *([cache_kt] * n_pages), *([cache_v2] * n_pages), past_bias, new_bias, sel, tok_of_row, lamv, sub2)
    return out.reshape(nb, D_MODEL)


ROW_CHUNKS = D_MODEL // LANES


def _router_kernel(*refs, tile_starts):
    n_groups = len(tile_starts)
    x_refs = refs[:n_groups]
    g_ref, w_ref, b_ref, h_ref, r_ref = refs[n_groups:]
    x = x_refs[0][...]
    for grp in range(1, n_groups):
        x = jnp.where(pl.program_id(0) >= tile_starts[grp], x_refs[grp][...], x)
    h = _rmsnorm(x, g_ref[...])
    tm = h.shape[0]
    for c in range(ROW_CHUNKS):
        h_ref[pl.ds(c, tm, stride=ROW_CHUNKS), :] = h[:, c * LANES:(c + 1) * LANES]
    logits = jnp.dot(h, w_ref[...], precision=lax.Precision.HIGHEST, preferred_element_type=F32) + b_ref[...]
    lane = lax.broadcasted_iota(jnp.int32, logits.shape, 1)
    neg = jnp.float32(-jnp.inf)
    big = jnp.int32(2 ** 30)

    def first_argmax(vals):
        mx = jnp.max(vals, axis=1, keepdims=True)
        idx = jnp.min(jnp.where(vals == mx, lane, big), axis=1, keepdims=True)
        return mx, idx

    lg = jnp.where(lane < N_GROUPS, logits, neg)
    g_max, grp = first_argmax(lg)
    w_grp = 1.0 / jnp.sum(jnp.exp(lg - g_max), axis=1, keepdims=True)
    e_lane = lane - N_GROUPS
    in_grp = (e_lane >= grp * EXPERTS_PER_GROUP) & (e_lane < (grp + 1) * EXPERTS_PER_GROUP)
    le = jnp.where(in_grp, logits, neg)
    m1, i1 = first_argmax(le)
    m2, i2 = first_argmax(jnp.where(lane == i1, neg, le))
    e2 = jnp.exp(m2 - m1)
    gate1 = w_grp / (1.0 + e2)
    gate2 = w_grp * e2 / (1.0 + e2)
    f = lambda a: a.astype(F32)
    r_ref[...] = jnp.where(lane == 0, f(i1 - N_GROUPS),
                           jnp.where(lane == 1, f(i2 - N_GROUPS),
                                     jnp.where(lane == 2, gate1, jnp.where(lane == 3, gate2, 0.0))))


def moe_router(xs, g, w_route, b_route, tm=512):
    d = xs[0].shape[1]
    xs = [x if x.shape[0] % tm == 0 else jnp.pad(x, ((0, -x.shape[0] % tm), (0, 0))) for x in xs]
    n_tiles = [x.shape[0] // tm for x in xs]
    tile_starts = tuple(np.cumsum([0] + n_tiles[:-1]).tolist())
    total = sum(n_tiles) * tm

    def group_spec(start, n):
        return pl.BlockSpec((tm, d), lambda i: (jnp.clip(i - start, 0, n - 1), 0))

    h, route = pl.pallas_call(
        functools.partial(_router_kernel, tile_starts=tile_starts),
        out_shape=(jax.ShapeDtypeStruct((total * ROW_CHUNKS, LANES), F32),
                   jax.ShapeDtypeStruct((total, ROUTE_LANES), F32)),
        grid=(sum(n_tiles),),
        in_specs=[group_spec(s, n) for s, n in zip(tile_starts, n_tiles)]
                 + [pl.BlockSpec((1, d), lambda i: (0, 0)),
                    pl.BlockSpec((d, ROUTE_LANES), lambda i: (0, 0)),
                    pl.BlockSpec((1, ROUTE_LANES), lambda i: (0, 0))],
        out_specs=(pl.BlockSpec((tm * ROW_CHUNKS, LANES), lambda i: (i, 0)),
                   pl.BlockSpec((tm, ROUTE_LANES), lambda i: (i, 0))),
        compiler_params=_params("parallel"),
        name="moe_router",
    )(*xs, g.reshape(1, d), w_route, b_route)
    return h, route, [s * tm for s in tile_starts]


GATHER_GROUP = 8


def _expert_kernel(be_ref, nu_ref, nv_ref, tok_cur_ref, tok_next_ref, h_hbm, wg_ref, wu_ref, wd_ref, o_ref,
                   x_buf, x_sem, wg_s, wu_s, wd_s, *, tb):
    b = pl.program_id(0)
    n_blk = pl.num_programs(0)
    slot = lax.rem(b, 2)

    def start_row(tok_ref, dst_slot, r, priority):
        src_row = pl.multiple_of(tok_ref[0, r] * ROW_CHUNKS, ROW_CHUNKS)
        dst_row = pl.multiple_of(r * ROW_CHUNKS, ROW_CHUNKS)
        pltpu.make_async_copy(h_hbm.at[pl.ds(src_row, ROW_CHUNKS), :],
                              x_buf.at[dst_slot, pl.ds(dst_row, ROW_CHUNKS), :],
                              x_sem.at[dst_slot]).start(priority=priority)

    def start_gather(tok_ref, dst_slot, n_rows):
        def group(gi, carry):
            for j in range(GATHER_GROUP):
                start_row(tok_ref, dst_slot, gi * GATHER_GROUP + j, j % 2)
            return carry
        lax.fori_loop(0, n_rows // GATHER_GROUP, group, 0)

    def wait_gather(dst_slot, n_rows):
        @pl.when(n_rows > 0)
        def _():
            n = pl.multiple_of(n_rows * ROW_CHUNKS, ROW_CHUNKS)
            pltpu.make_async_copy(h_hbm.at[pl.ds(0, n), :], x_buf.at[dst_slot, pl.ds(0, n), :],
                                  x_sem.at[dst_slot]).wait()

    def load_block(src_slot):
        return jnp.concatenate([x_buf[src_slot, pl.ds(c, tb, stride=ROW_CHUNKS), :] for c in range(ROW_CHUNKS)],
                               axis=1).astype(BF16)

    @pl.when(b == 0)
    def _first():
        x_buf[...] = jnp.zeros_like(x_buf)
        start_gather(tok_cur_ref, 0, nv_ref[0])

    @pl.when(b + 1 < n_blk)
    def _prefetch():
        start_gather(tok_next_ref, 1 - slot, nv_ref[jnp.minimum(b + 1, n_blk - 1)])

    wait_gather(slot, nv_ref[b])
    prev = be_ref[jnp.maximum(b - 1, 0)]

    @pl.when((b == 0) | (be_ref[b] != prev))
    def _load_expert():
        wg_s[...] = wg_ref[...].astype(BF16)
        wu_s[...] = wu_ref[...].astype(BF16)
        wd_s[...] = wd_ref[...].astype(BF16)

    @pl.when(b < nu_ref[0])
    def _compute():
        x = load_block(slot)
        mid = _silu(_dot(x, wg_s[...])) * _dot(x, wu_s[...])
        o_ref[...] = _dot(mid.astype(BF16), wd_s[...])

    @pl.when(b >= nu_ref[0])
    def _unused():
        o_ref[...] = jnp.zeros_like(o_ref)


def moe_experts(h, row_tok, blk_e, n_used, blk_fill, w_gate, w_up, w_down, layer, tb=MOE_TILE):
    d = D_MODEL
    n_blk = row_tok.shape[0] // tb
    tok3 = row_tok.reshape(n_blk, 1, tb)
    w_in_spec = pl.BlockSpec((None, None, d, D_FF), lambda b, be, nu, nv: (layer, be[b], 0, 0))
    tok_spec = lambda nxt: pl.BlockSpec((None, 1, tb),
                                        lambda b, be, nu, nv: (jnp.minimum(b + nxt, n_blk - 1), 0, 0),
                                        memory_space=pltpu.SMEM)
    grid_spec = pltpu.PrefetchScalarGridSpec(
        num_scalar_prefetch=3,
        grid=(n_blk,),
        in_specs=[tok_spec(0), tok_spec(1),
                  pl.BlockSpec(memory_space=pl.ANY),
                  w_in_spec, w_in_spec,
                  pl.BlockSpec((None, None, D_FF, d), lambda b, be, nu, nv: (layer, be[b], 0, 0))],
        out_specs=pl.BlockSpec((tb, d), lambda b, be, nu, nv: (b, 0)),
        scratch_shapes=[pltpu.VMEM((2, tb * ROW_CHUNKS, LANES), F32), pltpu.SemaphoreType.DMA((2,)),
                        pltpu.VMEM((d, D_FF), BF16), pltpu.VMEM((d, D_FF), BF16), pltpu.VMEM((D_FF, d), BF16)],
    )
    return pl.pallas_call(
        functools.partial(_expert_kernel, tb=tb),
        out_shape=jax.ShapeDtypeStruct((n_blk * tb, d), F32),
        grid_spec=grid_spec,
        compiler_params=_params("arbitrary"),
        name="moe_experts",
    )(blk_e, n_used, blk_fill, tok3, tok3, h, w_gate, w_up, w_down)


def _dispatch_plan(experts, tb):
    t, k_sel = experts.shape
    n_asg = t * k_sel
    flat_e = experts.reshape(n_asg)
    onehot = (flat_e[:, None] == jnp.arange(N_EXPERTS, dtype=jnp.int32)[None, :]).astype(jnp.int32)
    csum = jnp.cumsum(onehot, axis=0)
    counts = csum[-1]
    starts = jnp.cumsum(counts) - counts
    padded = (counts + tb - 1) // tb * tb
    pad_end = jnp.cumsum(padded)
    pad_start = pad_end - padded
    dest = jnp.sum(onehot * (csum - 1 + pad_start[None, :]), axis=1)
    n_blk = (n_asg + N_EXPERTS * (tb - 1) + tb - 1) // tb
    blk_row0 = jnp.arange(n_blk, dtype=jnp.int32) * tb
    blk_e = jnp.minimum(jnp.sum((blk_row0[:, None] >= pad_end[None, :]).astype(jnp.int32), axis=1),
                        N_EXPERTS - 1)
    order = jnp.argsort(flat_e).astype(jnp.int32)
    within = (blk_row0 - pad_start[blk_e])[:, None] + jnp.arange(tb, dtype=jnp.int32)[None, :]
    src = jnp.clip(starts[blk_e][:, None] + within, 0, n_asg - 1)
    row_tok = jnp.where(within < counts[blk_e][:, None], order[src.reshape(-1)].reshape(n_blk, tb) // k_sel, 0)
    n_used = (pad_end[-1:] // tb).astype(jnp.int32)
    filled = jnp.clip(counts[blk_e] - (blk_row0 - pad_start[blk_e]), 0, tb)
    blk_fill = ((filled + GATHER_GROUP - 1) // GATHER_GROUP * GATHER_GROUP).astype(jnp.int32)
    return row_tok.reshape(-1), dest.reshape(t, k_sel), blk_e.astype(jnp.int32), n_used, blk_fill


def _combine_ple_kernel(x_ref, y0_ref, y1_ref, r_ref, p_ref, g_ref, wg_ref, wp_ref, o_ref):
    route = r_ref[...]
    x = x_ref[...] + route[:, 2:3] * y0_ref[...] + route[:, 3:4] * y1_ref[...]
    h = _rmsnorm(x, g_ref[...]).astype(BF16)
    gate = _sigmoid(_dot(h, wg_ref[...]))
    proj = _dot(p_ref[...].astype(BF16), wp_ref[...])
    o_ref[...] = x + gate * proj


def moe_combine_ple(x, y0, y1, route, row_offset, p, g, wg_bf16, wp_bf16, tm=512):
    t, d = x.shape
    tm = _row_tile(t, tm)
    assert row_offset % tm == 0
    off = row_offset // tm
    pd = p.shape[1]
    row = pl.BlockSpec((tm, d), lambda i: (i, 0))
    row_off = pl.BlockSpec((tm, d), lambda i: (i + off, 0))
    return pl.pallas_call(
        _combine_ple_kernel,
        out_shape=jax.ShapeDtypeStruct((t, d), F32),
        grid=(t // tm,),
        in_specs=[row, row_off, row_off,
                  pl.BlockSpec((tm, ROUTE_LANES), lambda i: (i + off, 0)),
                  pl.BlockSpec((tm, pd), lambda i: (i, 0)),
                  pl.BlockSpec((1, d), lambda i: (0, 0)),
                  pl.BlockSpec((d, d), lambda i: (0, 0)),
                  pl.BlockSpec((pd, d), lambda i: (0, 0))],
        out_specs=row,
        compiler_params=_params("parallel"),
        name="moe_combine_ple",
    )(x, y0, y1, route, p, g.reshape(1, d), wg_bf16, wp_bf16)


def moe_ple_add(xs, ps, layer, norm_g, rg_w, rg_b, re_w, re_b, w_gate, w_up, w_down, ple_g, wg_bf16, wp_bf16,
                tb=MOE_TILE):
    d = xs[0].shape[1]
    w_route = jnp.zeros((d, ROUTE_LANES), F32).at[:, :N_GROUPS].set(rg_w).at[:, N_GROUPS:N_GROUPS + N_EXPERTS].set(re_w)
    b_route = jnp.zeros((1, ROUTE_LANES), F32).at[0, :N_GROUPS].set(rg_b).at[0, N_GROUPS:N_GROUPS + N_EXPERTS].set(re_b)
    h, route, offsets = moe_router(xs, norm_g, w_route, b_route)
    row_tok, dest, blk_e, n_used, blk_fill = _dispatch_plan(route[:, 0:2].astype(jnp.int32), tb)
    yb = moe_experts(h, row_tok, blk_e, n_used, blk_fill, w_gate, w_up, w_down, layer, tb)
    y0, y1 = yb[dest[:, 0]], yb[dest[:, 1]]
    return [moe_combine_ple(x, y0, y1, route, off, p, ple_g, wg_bf16, wp_bf16)
            for x, p, off in zip(xs, ps, offsets)]


def _lambda_init(layer):
    return 0.8 - 0.6 * math.exp(-0.3 * layer)


def _permute_even_in(w_in):
    d = w_in.shape[0]
    lr0 = 2 * GLA_KEY_W + 2 * GLA_VAL_W
    return jnp.concatenate([w_in[:, :lr0], w_in[:, lr0 + GLA_RANK:], w_in[:, lr0:lr0 + GLA_RANK],
                            jnp.zeros((d, LANES - GLA_RANK), w_in.dtype)], axis=1).astype(BF16)


def kernel(x_prompt, x_sample, cache_k, cache_v, state_gla, state_conv, page_table, p_prompt, p_sample, rel_bias, norm_mix, norm_ffn, norm_ple, w_in_ev, w_lr_up, b_lr, gla_norm, conv_w, conv_b, conv_ln_g, conv_ln_b, w_out_ev, w_qkv, q_norm, k_norm, lam_q1, lam_k1, lam_q2, lam_k2, subln, w_out_od, router_g_w, router_g_b, router_e_w, router_e_b, moe_w_gate, moe_w_up, moe_w_down, ple_proj, ple_gate):
    n_bp, seq, d = x_prompt.shape
    n_bs = x_sample.shape[0]
    depth = norm_mix.shape[0]
    n_layers_odd, n_pool, page = cache_k.shape[:3]
    xp = x_prompt.reshape(n_bp * seq, d)
    xs = x_sample.reshape(n_bs, d)
    ck = jnp.transpose(cache_k, (0, 1, 3, 4, 5, 2)).reshape(n_layers_odd, n_pool, d, page)
    cv = cache_v.reshape(n_layers_odd, n_pool, page * DIFF_HEADS, DIFF_DV)
    prompt_bias = prompt_bias_tables(rel_bias)
    step_bias = step_bias_tables(rel_bias, page_table.shape[1] * page)
    step_rows = 16
    kp_l, vp_l, ks_l, vs_l, gp_l, gs_l, cp_l, cs_l = [], [], [], [], [], [], [], []
    for i in range(depth):
        j = i // 2
        if i % 2 == 0:
            w_in = _permute_even_in(w_in_ev[j])
            w_lr_pad = jnp.zeros((LANES, GLA_KEY_W), BF16).at[:GLA_RANK].set(w_lr_up[j].astype(BF16))
            cw_pad = jnp.zeros((CONV_HALO, CONV_CH), F32).at[:CONV_WIDTH].set(conv_w[j])
            w_out = w_out_ev[j].astype(BF16)
            conv_args = (cw_pad, conv_b[j], conv_ln_g[j], conv_ln_b[j])
            zp = norm_proj(xp, norm_mix[i], w_in)
            op, g_p = gla_mix(zp, n_bp, seq, GLA_CHUNK, 512, w_lr_pad, b_lr[j], gla_norm[j], None)
            cp, c_p = conv_mix(zp, n_bp, seq, *conv_args)
            xp = proj_residual([op, cp], w_out, xp)
            zs = norm_proj(xs, norm_mix[i], w_in)
            zs_pad = jnp.zeros((n_bs, step_rows, EVEN_Z_W), F32).at[:, 0].set(zs).reshape(n_bs * step_rows, EVEN_Z_W)
            os_pad, g_s = gla_mix(zs_pad, n_bs, step_rows, step_rows, 8 * step_rows, w_lr_pad, b_lr[j], gla_norm[j],
                                  state_gla[j].reshape(n_bs, GLA_KEY_W, GLA_DV), valid_rows=1)
            os_ = os_pad.reshape(n_bs, step_rows, GLA_VAL_W)[:, 0]
            cs, c_s = conv_step(zs, state_conv[j], *conv_args)
            xs = proj_residual([os_, cs], w_out, xs)
            gp_l.append(g_p.reshape(n_bp, GLA_HEADS, GLA_DK, GLA_DV))
            gs_l.append(g_s.reshape(n_bs, GLA_HEADS, GLA_DK, GLA_DV))
            cp_l.append(c_p)
            cs_l.append(c_s)
        else:
            lam0 = _lambda_init(i)
            lamv = jnp.stack([lam_q1[j], lam_k1[j], lam_q2[j], lam_k2[j]]).astype(F32)
            wq = w_qkv[j].astype(BF16)
            w_out = w_out_od[j].astype(BF16)
            qp, kp, vp, kpb, vpb = qkv_proj(xp, norm_mix[i], wq, q_norm[j], k_norm[j])
            ap = diff_attention(qp, kpb, vpb, n_bp, seq, prompt_bias, lamv, subln[j], lam0)
            xp = proj_residual([ap], w_out, xp)
            qs, ks, vs, _, _ = qkv_proj(xs, norm_mix[i], wq, q_norm[j], k_norm[j])
            as_ = diff_attention_step(qs, ks, vs, ck, cv, j, page_table, step_bias, lamv, subln[j], lam0)
            xs = proj_residual([as_], w_out, xs)
            kp_l.append(kp.reshape(n_bp, seq, DIFF_HEADS, 2, DIFF_DH))
            vp_l.append(vp.reshape(n_bp, seq, DIFF_HEADS, DIFF_DV))
            ks_l.append(ks.reshape(n_bs, 1, DIFF_HEADS, 2, DIFF_DH))
            vs_l.append(vs.reshape(n_bs, 1, DIFF_HEADS, DIFF_DV))
        tail_args = (i, norm_ffn[i], router_g_w[i], router_g_b[i], router_e_w[i], router_e_b[i],
                     moe_w_gate, moe_w_up, moe_w_down, norm_ple[i], ple_gate[i].astype(BF16), ple_proj[i].astype(BF16))
        xp, xs = moe_ple_add([xp, xs], [p_prompt[i].reshape(n_bp * seq, PLE_DIM), p_sample[i].reshape(n_bs, PLE_DIM)],
                             *tail_args)
    return (xp.reshape(n_bp, seq, d), xs.reshape(n_bs, 1, d),
            jnp.stack(kp_l), jnp.stack(vp_l), jnp.stack(ks_l), jnp.stack(vs_l),
            jnp.stack(gp_l), jnp.stack(gs_l), jnp.stack(cp_l), jnp.stack(cs_l))
```

```python
import functools
import math

import numpy as np
import jax
import jax.numpy as jnp
from jax import lax
from jax.experimental import pallas as pl
from jax.experimental.pallas import tpu as pltpu

F32 = jnp.float32
BF16 = jnp.bfloat16

D_MODEL = 1024
NORM_EPS = 1e-6
GLA_HEADS = 4
GLA_DK = 64
GLA_DV = 128
GLA_KEY_W = GLA_HEADS * GLA_DK
GLA_VAL_W = GLA_HEADS * GLA_DV
GLA_RANK = 16
GLA_TAU = 16.0
GLA_CHUNK = 64
CONV_CH = 512
CONV_WIDTH = 31
CONV_HALO = 32
DIFF_HEADS = 8
DIFF_DH = 64
DIFF_DV = 128
REL_BUCKETS = 32
REL_MAX_DIST = 128
ATT_TILE = 512
MASK_VALUE = -1e30
N_GROUPS = 4
EXPERTS_PER_GROUP = 8
N_EXPERTS = N_GROUPS * EXPERTS_PER_GROUP
D_FF = 512
MOE_TILE = 256
ROUTE_LANES = 128
PLE_DIM = 256

LANES = 128
VMEM_LIMIT_BYTES = 56 * 1024 * 1024
EVEN_Z_W = 2 * GLA_KEY_W + 2 * GLA_VAL_W + 2 * CONV_CH + LANES
EVEN_LR_BLOCK = (2 * GLA_KEY_W + 2 * GLA_VAL_W + 2 * CONV_CH) // LANES


def _params(*sem):
    return pltpu.CompilerParams(dimension_semantics=sem, vmem_limit_bytes=VMEM_LIMIT_BYTES)


def _dot(a, b):
    return jnp.dot(a, b, preferred_element_type=F32)


def _dot_nt(a, b):
    return lax.dot_general(a, b, (((1,), (1,)), ((), ())), preferred_element_type=F32)


def _dot_tn(a, b):
    return lax.dot_general(a, b, (((0,), (0,)), ((), ())), preferred_element_type=F32)


def _split_bf16(x):
    hi = x.astype(BF16)
    lo = (x - hi.astype(F32)).astype(BF16)
    return hi, lo


def _rmsnorm(x, g):
    ms = jnp.mean(x * x, axis=-1, keepdims=True)
    return x * lax.rsqrt(ms + NORM_EPS) * g


def _sigmoid(x):
    return 1.0 / (1.0 + jnp.exp(-x))


def _silu(x):
    return x * _sigmoid(x)


def _row_tile(n_rows, want):
    t = min(want, n_rows)
    assert n_rows % t == 0, (n_rows, t)
    return t


def _norm_proj_kernel(x_ref, g_ref, w_ref, o_ref):
    h = _rmsnorm(x_ref[...], g_ref[...]).astype(BF16)
    o_ref[...] = _dot(h, w_ref[...])


def norm_proj(x, g, w_bf16, tm=256):
    t, d = x.shape
    n = w_bf16.shape[1]
    tm = _row_tile(t, tm)
    return pl.pallas_call(
        _norm_proj_kernel,
        out_shape=jax.ShapeDtypeStruct((t, n), F32),
        grid=(t // tm,),
        in_specs=[pl.BlockSpec((tm, d), lambda i: (i, 0)),
                  pl.BlockSpec((1, d), lambda i: (0, 0)),
                  pl.BlockSpec((d, n), lambda i: (0, 0))],
        out_specs=pl.BlockSpec((tm, n), lambda i: (i, 0)),
        compiler_params=_params("parallel"),
        name="norm_proj",
    )(x, g.reshape(1, d), w_bf16)


def _proj_residual_kernel(*refs, n_in):
    a_refs = refs[:n_in]
    w_ref, x_ref, o_ref = refs[n_in:]
    acc = x_ref[...]
    row = 0
    for a_ref in a_refs:
        k = a_ref.shape[1]
        acc = acc + _dot(a_ref[...].astype(BF16), w_ref[row:row + k, :])
        row += k
    o_ref[...] = acc


def proj_residual(acts, w_bf16, x, tm=512):
    t, d = x.shape
    tm = _row_tile(t, tm)
    n_in = len(acts)
    in_specs = [pl.BlockSpec((tm, a.shape[1]), lambda i: (i, 0)) for a in acts]
    in_specs += [pl.BlockSpec(w_bf16.shape, lambda i: (0, 0)),
                 pl.BlockSpec((tm, d), lambda i: (i, 0))]
    return pl.pallas_call(
        functools.partial(_proj_residual_kernel, n_in=n_in),
        out_shape=jax.ShapeDtypeStruct((t, d), F32),
        grid=(t // tm,),
        in_specs=in_specs,
        out_specs=pl.BlockSpec((tm, d), lambda i: (i, 0)),
        compiler_params=_params("parallel"),
        name="proj_residual",
    )(*acts, w_bf16, x)


def _ple_kernel(x_ref, p_ref, g_ref, wg_ref, wp_ref, o_ref):
    x = x_ref[...]
    h = _rmsnorm(x, g_ref[...]).astype(BF16)
    gate = _sigmoid(_dot(h, wg_ref[...]))
    proj = _dot(p_ref[...].astype(BF16), wp_ref[...])
    o_ref[...] = x + gate * proj


def ple_add(x, p, g, wg_bf16, wp_bf16, tm=512):
    t, d = x.shape
    tm = _row_tile(t, tm)
    pd = p.shape[1]
    return pl.pallas_call(
        _ple_kernel,
        out_shape=jax.ShapeDtypeStruct((t, d), F32),
        grid=(t // tm,),
        in_specs=[pl.BlockSpec((tm, d), lambda i: (i, 0)),
                  pl.BlockSpec((tm, pd), lambda i: (i, 0)),
                  pl.BlockSpec((1, d), lambda i: (0, 0)),
                  pl.BlockSpec((d, d), lambda i: (0, 0)),
                  pl.BlockSpec((pd, d), lambda i: (0, 0))],
        out_specs=pl.BlockSpec((tm, d), lambda i: (i, 0)),
        compiler_params=_params("parallel"),
        name="ple_add",
    )(x, p, g.reshape(1, d), wg_bf16, wp_bf16)


def _gla_constants(chunk):
    c = chunk
    t = np.arange(c)
    tri = (t[None, :] <= t[:, None]).astype(np.float32)
    causal4 = np.tile(tri, (GLA_HEADS, 1))
    lane_head = np.arange(GLA_KEY_W) // GLA_DK
    row_head = np.repeat(np.arange(GLA_HEADS), c)
    headmask4 = (row_head[:, None] == lane_head[None, :]).astype(np.float32)
    blockdiag = (np.arange(GLA_KEY_W)[:, None] // GLA_DK
                 == np.arange(GLA_VAL_W)[None, :] // GLA_DV).astype(np.float32)
    return (jnp.asarray(tri, BF16), jnp.asarray(causal4), jnp.asarray(headmask4), jnp.asarray(blockdiag))


def _gla_kernel(*refs, chunk, n_chunks, has_state, valid_rows, chunk_is_sequence):
    if has_state:
        (qk_ref, v_ref, r_ref, lr_ref, wlr_ref, blr_ref, gg_ref, tri_ref, causal_ref, hmask_ref, bd_ref,
         s0_ref, o_ref, sout_ref, s_scr) = refs
    else:
        (qk_ref, v_ref, r_ref, lr_ref, wlr_ref, blr_ref, gg_ref, tri_ref, causal_ref, hmask_ref, bd_ref,
         o_ref, sout_ref, s_scr) = refs
    c = chunk
    i = pl.program_id(1)
    bd = bd_ref[...]

    def to_block_diag(s):
        return jnp.concatenate([s] * GLA_HEADS, axis=1) * bd

    def store_state(s_bd, slot):
        for h in range(GLA_HEADS):
            sout_ref[slot, h * GLA_DK:(h + 1) * GLA_DK, :] = s_bd[h * GLA_DK:(h + 1) * GLA_DK,
                                                                  h * GLA_DV:(h + 1) * GLA_DV]

    if not chunk_is_sequence:
        @pl.when(i == 0)
        def _init():
            if has_state:
                s_scr[...] = to_block_diag(s0_ref[0])
            else:
                s_scr[...] = jnp.zeros_like(s_scr)

    tri = tri_ref[...]
    ones = jnp.ones((c, LANES), BF16)

    def chunk_step(ci, carry):
        rows = pl.ds(pl.multiple_of(ci * c, c), c)
        qk = qk_ref[rows, :]
        q = qk[:, :GLA_KEY_W] * (GLA_DK ** -0.5)
        k = qk[:, GLA_KEY_W:]
        vb = v_ref[rows, :].astype(BF16)
        x = _dot(lr_ref[rows, :].astype(BF16), wlr_ref[...]) + blr_ref[...]
        g = (jnp.minimum(x, 0.0) - jnp.log1p(jnp.exp(-jnp.abs(x)))) * (1.0 / GLA_TAU)
        if valid_rows < c:
            g = jnp.where(lax.broadcasted_iota(jnp.int32, g.shape, 0) < valid_rows, g, 0.0)
        g_hi, g_lo = _split_bf16(g)
        b = _dot(tri, g_hi) + _dot(tri, g_lo)
        dcol = _dot_tn(g_hi, ones) + _dot_tn(g_lo, ones)
        bend = b[c - 1:c, :]
        half = 0.5 * bend
        qs = q * jnp.exp(b - half)
        ks = k * jnp.exp(half - b)
        qt = q * jnp.exp(b)
        kh = k * jnp.exp(bend - b)
        qstack = (jnp.concatenate([qs] * GLA_HEADS, axis=0) * hmask_ref[...]).astype(BF16)
        scores = (_dot_nt(qstack, ks.astype(BF16)) * causal_ref[...]).astype(BF16)
        s_bd = to_block_diag(s0_ref[ci]) if chunk_is_sequence else s_scr[...]
        o_inter = _dot(qt.astype(BF16), s_bd.astype(BF16))
        gg = gg_ref[...]
        for h in range(GLA_HEADS):
            cols = slice(h * GLA_DV, (h + 1) * GLA_DV)
            o_h = _dot(scores[h * c:(h + 1) * c, :], vb[:, cols]) + o_inter[:, cols]
            y = _rmsnorm(o_h, gg) * _silu(r_ref[rows, cols])
            o_ref[rows, cols] = y.astype(o_ref.dtype)
        decay = jnp.concatenate([jnp.exp(dcol)] * GLA_HEADS, axis=1)
        s_new = s_bd * decay + _dot_tn(kh.astype(BF16), vb) * bd
        if chunk_is_sequence:
            store_state(s_new, ci)
        else:
            s_scr[...] = s_new
        return carry

    lax.fori_loop(0, n_chunks, chunk_step, 0, unroll=2 if n_chunks % 2 == 0 else 1)

    if not chunk_is_sequence:
        @pl.when(i == pl.num_programs(1) - 1)
        def _final():
            store_state(s_scr[...], 0)


def gla_mix(z, n_batch, seq, chunk, tile, w_lr_pad, b_lr, gla_g, state0, valid_rows=None):
    chunk_is_sequence = seq == chunk and tile > seq
    if chunk_is_sequence:
        assert state0 is not None and n_batch % (tile // seq) == 0
        seqs = tile // seq
        n_groups, nt = n_batch // seqs, 1
    else:
        assert seq % tile == 0 and tile % chunk == 0
        seqs, n_groups, nt = 1, n_batch, seq // tile
    valid_rows = chunk if valid_rows is None else valid_rows
    assert valid_rows == chunk or seq == chunk
    has_state = state0 is not None
    consts = _gla_constants(chunk)
    row = lambda b, i: b * nt + i
    in_specs = [pl.BlockSpec((tile, 2 * GLA_KEY_W), lambda b, i: (row(b, i), 0)),
                pl.BlockSpec((tile, GLA_VAL_W), lambda b, i: (row(b, i), 1)),
                pl.BlockSpec((tile, GLA_VAL_W), lambda b, i: (row(b, i), 2)),
                pl.BlockSpec((tile, LANES), lambda b, i: (row(b, i), EVEN_LR_BLOCK)),
                pl.BlockSpec(w_lr_pad.shape, lambda b, i: (0, 0)),
                pl.BlockSpec((1, GLA_KEY_W), lambda b, i: (0, 0)),
                pl.BlockSpec((1, GLA_DV), lambda b, i: (0, 0))]
    in_specs += [pl.BlockSpec(a.shape, lambda b, i: (0, 0)) for a in consts]
    args = [z, z, z, z, w_lr_pad, b_lr.reshape(1, GLA_KEY_W), gla_g.reshape(1, GLA_DV), *consts]
    state_spec = pl.BlockSpec((seqs, GLA_KEY_W, GLA_DV), lambda b, i: (b, 0, 0))
    if has_state:
        in_specs.append(state_spec)
        args.append(state0)
    return pl.pallas_call(
        functools.partial(_gla_kernel, chunk=chunk, n_chunks=tile // chunk, has_state=has_state,
                          valid_rows=valid_rows, chunk_is_sequence=chunk_is_sequence),
        out_shape=(jax.ShapeDtypeStruct((n_batch * seq, GLA_VAL_W), BF16),
                   jax.ShapeDtypeStruct((n_batch, GLA_KEY_W, GLA_DV), F32)),
        grid=(n_groups, nt),
        in_specs=in_specs,
        out_specs=(pl.BlockSpec((tile, GLA_VAL_W), lambda b, i: (row(b, i), 0)), state_spec),
        scratch_shapes=[pltpu.VMEM((GLA_KEY_W, GLA_VAL_W), F32)],
        compiler_params=_params("parallel", "arbitrary"),
        name="gla_mix",
    )(*args)


def _layernorm_silu(c, g, b):
    mu = jnp.mean(c, axis=-1, keepdims=True)
    cc = c - mu
    var = jnp.mean(cc * cc, axis=-1, keepdims=True)
    return _silu(cc * lax.rsqrt(var + NORM_EPS) * g + b)


def _conv_kernel(a_ref, gt_ref, cw_ref, cb_ref, lg_ref, lb_ref, o_ref, st_ref, ext, shifted, *, tile, sub):
    i = pl.program_id(1)
    n_state = CONV_WIDTH - 1
    sublanes = 8
    span = tile + CONV_HALO - sublanes

    @pl.when(i == 0)
    def _zero_halo():
        ext[0:CONV_HALO, :] = jnp.zeros((CONV_HALO, CONV_CH), F32)

    @pl.when(i > 0)
    def _carry_halo():
        ext[0:CONV_HALO, :] = ext[tile:tile + CONV_HALO, :]

    ext[CONV_HALO:CONV_HALO + tile, :] = a_ref[...] * _sigmoid(gt_ref[...])
    for ph in range(1, sublanes):
        shifted[ph, 0:span, :] = ext[ph:ph + span, :]
    lg = lg_ref[...]
    lb = lb_ref[...]
    for s in range(tile // sub):
        acc = jnp.broadcast_to(cb_ref[...], (sub, CONV_CH))
        for w in range(CONV_WIDTH):
            off = (CONV_HALO - n_state) + w
            ph, base = off % sublanes, s * sub + off - off % sublanes
            rows = ext[base:base + sub, :] if ph == 0 else shifted[ph, base:base + sub, :]
            acc = acc + rows * cw_ref[w:w + 1, :]
        o_ref[s * sub:(s + 1) * sub, :] = _layernorm_silu(acc, lg, lb).astype(o_ref.dtype)

    @pl.when(i == pl.num_programs(1) - 1)
    def _state():
        st_ref[0] = ext[tile + CONV_HALO - n_state:tile + CONV_HALO, :]


def conv_mix(z, n_batch, seq, conv_w_pad, conv_b, ln_g, ln_b, tile=256, sub=32):
    nt = seq // tile
    assert seq % tile == 0 and tile % sub == 0 and tile >= CONV_HALO
    row = lambda b, i: b * nt + i
    vec = lambda a: a.reshape(1, CONV_CH)
    return pl.pallas_call(
        functools.partial(_conv_kernel, tile=tile, sub=sub),
        out_shape=(jax.ShapeDtypeStruct((n_batch * seq, CONV_CH), BF16),
                   jax.ShapeDtypeStruct((n_batch, CONV_WIDTH - 1, CONV_CH), F32)),
        grid=(n_batch, nt),
        in_specs=[pl.BlockSpec((tile, CONV_CH), lambda b, i: (row(b, i), 3)),
                  pl.BlockSpec((tile, CONV_CH), lambda b, i: (row(b, i), 4)),
                  pl.BlockSpec(conv_w_pad.shape, lambda b, i: (0, 0)),
                  pl.BlockSpec((1, CONV_CH), lambda b, i: (0, 0)),
                  pl.BlockSpec((1, CONV_CH), lambda b, i: (0, 0)),
                  pl.BlockSpec((1, CONV_CH), lambda b, i: (0, 0))],
        out_specs=(pl.BlockSpec((tile, CONV_CH), lambda b, i: (row(b, i), 0)),
                   pl.BlockSpec((1, CONV_WIDTH - 1, CONV_CH), lambda b, i: (b, 0, 0))),
        scratch_shapes=[pltpu.VMEM((tile + CONV_HALO, CONV_CH), F32),
                        pltpu.VMEM((8, tile + CONV_HALO, CONV_CH), F32)],
        compiler_params=_params("parallel", "arbitrary"),
        name="conv_mix",
    )(z, z, conv_w_pad, vec(conv_b), vec(ln_g), vec(ln_b))


def _conv_step_kernel(a_ref, gt_ref, st_ref, cw_ref, cb_ref, lg_ref, lb_ref, o_ref, sto_ref):
    n_state = CONV_WIDTH - 1
    u = a_ref[...] * _sigmoid(gt_ref[...])
    acc = cb_ref[...] + u * cw_ref[n_state:n_state + 1, :]
    for w in range(n_state):
        acc = acc + st_ref[:, w * CONV_CH:(w + 1) * CONV_CH] * cw_ref[w:w + 1, :]
    o_ref[...] = _layernorm_silu(acc, lg_ref[...], lb_ref[...]).astype(o_ref.dtype)
    sto_ref[:, 0:(n_state - 1) * CONV_CH] = st_ref[:, CONV_CH:n_state * CONV_CH]
    sto_ref[:, (n_state - 1) * CONV_CH:] = u


def conv_step(z, state, conv_w_pad, conv_b, ln_g, ln_b, tile=32):
    nb = z.shape[0]
    n_state = CONV_WIDTH - 1
    tile = _row_tile(nb, tile)
    vec = lambda a: a.reshape(1, CONV_CH)
    st2 = state.reshape(nb, n_state * CONV_CH)
    out, st_new = pl.pallas_call(
        _conv_step_kernel,
        out_shape=(jax.ShapeDtypeStruct((nb, CONV_CH), BF16),
                   jax.ShapeDtypeStruct((nb, n_state * CONV_CH), F32)),
        grid=(nb // tile,),
        in_specs=[pl.BlockSpec((tile, CONV_CH), lambda i: (i, 3)),
                  pl.BlockSpec((tile, CONV_CH), lambda i: (i, 4)),
                  pl.BlockSpec((tile, n_state * CONV_CH), lambda i: (i, 0)),
                  pl.BlockSpec(conv_w_pad.shape, lambda i: (0, 0)),
                  pl.BlockSpec((1, CONV_CH), lambda i: (0, 0)),
                  pl.BlockSpec((1, CONV_CH), lambda i: (0, 0)),
                  pl.BlockSpec((1, CONV_CH), lambda i: (0, 0))],
        out_specs=(pl.BlockSpec((tile, CONV_CH), lambda i: (i, 0)),
                   pl.BlockSpec((tile, n_state * CONV_CH), lambda i: (i, 0))),
        compiler_params=_params("parallel"),
        name="conv_step",
    )(z, z, st2, conv_w_pad, vec(conv_b), vec(ln_g), vec(ln_b))
    return out, st_new.reshape(nb, n_state, CONV_CH)


def _group_constants():
    grp = np.zeros((D_MODEL, LANES), np.float32)
    grp[np.arange(D_MODEL), np.arange(D_MODEL) // DIFF_DH] = 1.0
    return jnp.asarray(grp, BF16), jnp.asarray(grp.T.copy(), BF16)


def _qkv_kernel(x_ref, g_ref, w_ref, grp_ref, grpt_ref, qg_ref, kg_ref, q_ref, k_ref, v_ref, kb_ref, vb_ref):
    h = _rmsnorm(x_ref[...], g_ref[...]).astype(BF16)
    z = _dot(h, w_ref[...])
    grp = grp_ref[...]
    grpt = grpt_ref[...]

    def map_norm(y, gain):
        hi, lo = _split_bf16(y * y)
        ms = (_dot(hi, grp) + _dot(lo, grp)) * (1.0 / DIFF_DH)
        rh, rl = _split_bf16(lax.rsqrt(ms + NORM_EPS))
        return y * (_dot(rh, grpt) + _dot(rl, grpt)) * gain

    qn = map_norm(z[:, :D_MODEL], qg_ref[...])
    kn = map_norm(z[:, D_MODEL:2 * D_MODEL], kg_ref[...])
    v = z[:, 2 * D_MODEL:]
    q_ref[...] = (qn * (DIFF_DH ** -0.5)).astype(BF16)
    k_ref[...] = kn
    v_ref[...] = v
    kb_ref[...] = kn.astype(BF16)
    ones = jnp.ones((v.shape[0], DIFF_DV), BF16)
    for h in range(DIFF_HEADS):
        vb_ref[:, 2 * h * DIFF_DV:(2 * h + 1) * DIFF_DV] = v[:, h * DIFF_DV:(h + 1) * DIFF_DV].astype(BF16)
        vb_ref[:, (2 * h + 1) * DIFF_DV:(2 * h + 2) * DIFF_DV] = ones


def qkv_proj(x, g, w_bf16, q_gain, k_gain, tm=256):
    t, d = x.shape
    tm = _row_tile(t, tm)
    grp, grpt = _group_constants()
    tile_gain = lambda a: jnp.tile(a, D_MODEL // DIFF_DH).reshape(1, D_MODEL)
    row = pl.BlockSpec((tm, d), lambda i: (i, 0))
    full = lambda a: pl.BlockSpec(a.shape, lambda i: (0, 0))
    qg, kg = tile_gain(q_gain), tile_gain(k_gain)
    return pl.pallas_call(
        _qkv_kernel,
        out_shape=(jax.ShapeDtypeStruct((t, d), BF16), jax.ShapeDtypeStruct((t, d), F32),
                   jax.ShapeDtypeStruct((t, d), F32), jax.ShapeDtypeStruct((t, d), BF16),
                   jax.ShapeDtypeStruct((t, 2 * d), BF16)),
        grid=(t // tm,),
        in_specs=[row, pl.BlockSpec((1, d), lambda i: (0, 0)), full(w_bf16), full(grp), full(grpt),
                  full(qg), full(kg)],
        out_specs=(row, row, row, row, pl.BlockSpec((tm, 2 * d), lambda i: (i, 0))),
        compiler_params=_params("parallel"),
        name="qkv_proj",
    )(x, g.reshape(1, d), w_bf16, grp, grpt, qg, kg)


def _rel_bucket_np(n):
    max_exact = REL_BUCKETS // 2
    nf = np.maximum(n, 1).astype(np.float64)
    large = max_exact + (np.log(nf / max_exact) / math.log(REL_MAX_DIST / max_exact)
                         * (REL_BUCKETS - max_exact)).astype(np.int32)
    return np.where(n < max_exact, np.maximum(n, 0), np.minimum(large, REL_BUCKETS - 1))


def _diff_lambda(lamv, lam0):
    s1 = jnp.sum(lamv[0:1, :] * lamv[1:2, :], axis=-1, keepdims=True)
    s2 = jnp.sum(lamv[2:3, :] * lamv[3:4, :], axis=-1, keepdims=True)
    return jnp.exp(s1) - jnp.exp(s2) + lam0


def _attn_kernel(q_ref, k_ref, v_ref, bdiag_ref, bsub_ref, bfar_ref, lamv_ref, sub_ref, o_ref,
                 m_scr, acc_scr, *, tile, lam0):
    qi = pl.program_id(2)
    q = q_ref[...]
    lane = lax.broadcasted_iota(jnp.int32, q.shape, 1)
    zero = jnp.zeros_like(q)
    q_maps = (jnp.where(lane < DIFF_DH, q, zero), jnp.where(lane >= DIFF_DH, q, zero))
    reps = tile // LANES

    m_scr[...] = jnp.full(m_scr.shape, -jnp.inf, F32)
    acc_scr[...] = jnp.zeros(acc_scr.shape, F32)

    def flash_step(kt, bias):
        rows = pl.ds(pl.multiple_of(kt * tile, tile), tile)
        k = k_ref[rows, :]
        v = v_ref[rows, :]
        for m in range(2):
            s = _dot_nt(q_maps[m], k) + bias
            m_prev = m_scr[m]
            m_new = jnp.maximum(m_prev, jnp.max(s, axis=1, keepdims=True))
            alpha = jnp.exp(m_prev - m_new)
            p = jnp.exp(s - jnp.concatenate([m_new] * reps, axis=1))
            acc_scr[m] = jnp.concatenate([alpha, alpha], axis=1) * acc_scr[m] + _dot(p.astype(BF16), v)
            m_scr[m] = m_new

    n_far = jnp.maximum(qi - 1, 0)

    def far_pair(i, carry):
        flash_step(2 * i, bfar_ref[0])
        flash_step(2 * i + 1, bfar_ref[0])
        return carry

    lax.fori_loop(0, n_far // 2, far_pair, 0)

    @pl.when(lax.rem(n_far, 2) == 1)
    def _far_tail():
        flash_step(n_far - 1, bfar_ref[0])

    @pl.when(qi >= 1)
    def _sub_and_diag():
        flash_step(qi - 1, bsub_ref[0])
        flash_step(qi, bdiag_ref[0])

    @pl.when(qi == 0)
    def _diag_only():
        flash_step(qi, bdiag_ref[0])

    lam = _diff_lambda(lamv_ref[...], lam0)
    a0 = acc_scr[0]
    a1 = acc_scr[1]
    o = a0[:, :DIFF_DV] / a0[:, DIFF_DV:] - lam * (a1[:, :DIFF_DV] / a1[:, DIFF_DV:])
    o_ref[...] = (_rmsnorm(o, sub_ref[...]) * (1.0 - lam0)).astype(o_ref.dtype)


def _bucket_lookup(rel_bias, n):
    onehot = np.zeros((n.size, REL_BUCKETS), np.float32)
    onehot[np.arange(n.size), _rel_bucket_np(n).reshape(-1)] = 1.0
    out = jnp.dot(jnp.asarray(onehot), rel_bias.astype(F32), precision=lax.Precision.HIGHEST)
    return out.reshape(n.shape + (rel_bias.shape[1],))


def prompt_bias_tables(rel_bias, tile=ATT_TILE):
    r = np.arange(tile)
    n_diag = r[:, None] - r[None, :]
    bdiag = jnp.where(jnp.asarray(n_diag >= 0)[None], jnp.moveaxis(_bucket_lookup(rel_bias, n_diag), -1, 0),
                      MASK_VALUE)
    bsub = jnp.moveaxis(_bucket_lookup(rel_bias, n_diag + tile), -1, 0)
    assert tile + 1 >= REL_MAX_DIST
    bfar = jnp.broadcast_to(rel_bias[REL_BUCKETS - 1][:, None, None], (DIFF_HEADS, 1, tile))
    return bdiag.astype(F32), bsub.astype(F32), bfar.astype(F32)


def diff_attention(q, kb, vb1, n_batch, seq, bias_tables, lamv, subln, lam0, tile=ATT_TILE):
    nq = seq // tile
    assert seq % tile == 0
    bdiag, bsub, bfar = bias_tables
    head_tile = lambda b, h, i: (h, 0, 0)
    return pl.pallas_call(
        functools.partial(_attn_kernel, tile=tile, lam0=lam0),
        out_shape=jax.ShapeDtypeStruct((n_batch * seq, D_MODEL), BF16),
        grid=(n_batch, DIFF_HEADS, nq),
        in_specs=[pl.BlockSpec((tile, DIFF_DV), lambda b, h, i: (b * nq + i, h)),
                  pl.BlockSpec((seq, DIFF_DV), lambda b, h, i: (b, h)),
                  pl.BlockSpec((seq, 2 * DIFF_DV), lambda b, h, i: (b, h)),
                  pl.BlockSpec((1, tile, tile), head_tile),
                  pl.BlockSpec((1, tile, tile), head_tile),
                  pl.BlockSpec((1, 1, tile), head_tile),
                  pl.BlockSpec((4, DIFF_DH), lambda b, h, i: (0, 0)),
                  pl.BlockSpec((1, DIFF_DV), lambda b, h, i: (0, 0))],
        out_specs=pl.BlockSpec((tile, DIFF_DV), lambda b, h, i: (b * nq + i, h)),
        scratch_shapes=[pltpu.VMEM((2, tile, LANES), F32), pltpu.VMEM((2, tile, 2 * DIFF_DV), F32)],
        compiler_params=_params("parallel", "parallel", "arbitrary"),
        name="diff_attention",
    )(q, kb, vb1, bdiag, bsub, bfar, lamv, subln.reshape(1, DIFF_DV))


N_MAPS = 2 * DIFF_HEADS


def step_bias_tables(rel_bias, n_past):
    n = n_past - np.arange(n_past)
    past = _bucket_lookup(rel_bias, n).T
    new = jnp.broadcast_to(_bucket_lookup(rel_bias, np.zeros((1,), np.int64)).T, (DIFF_HEADS, LANES))
    return jnp.concatenate([past, past], axis=0), jnp.concatenate([new, new], axis=0)


def _step_constants(page):
    j = np.arange(N_MAPS)
    c = np.arange(D_MODEL)
    col_head, col_map = c // DIFF_DV, (c % DIFF_DV) // DIFF_DH
    sel = ((j[:, None] % DIFF_HEADS == col_head[None, :])
           & (j[:, None] // DIFF_HEADS == col_map[None, :])).astype(np.float32)
    r = np.arange(page * DIFF_HEADS)
    tok_of_row = (r[:, None] // DIFF_HEADS == np.arange(page)[None, :]).astype(np.float32)
    return jnp.asarray(sel), jnp.asarray(tok_of_row)


def _attn_step_kernel(pt_ref, q_ref, kn_ref, vn_ref, *refs, n_pages, page, lam0):
    kt_refs = refs[:n_pages]
    v_refs = refs[n_pages:2 * n_pages]
    (bias_ref, nbias_ref, sel_ref, tok_ref, lamv_ref, sub_ref, o_ref, s_scr) = refs[2 * n_pages:]
    sel = sel_ref[...]
    q_sel = sel * q_ref[0].astype(F32)
    q_bf = q_sel.astype(BF16)
    for p in range(n_pages):
        cols = slice(p * page, (p + 1) * page)
        s_scr[:, cols] = _dot(q_bf, kt_refs[p][...].astype(BF16)) + bias_ref[:, cols]
    s_new = jnp.sum(q_sel * kn_ref[0], axis=1, keepdims=True) + nbias_ref[:, 0:1]
    s = s_scr[...]
    m = jnp.maximum(jnp.max(s, axis=1, keepdims=True), s_new)
    p_all = jnp.exp(s - m)
    p_new = jnp.exp(s_new - m)
    l = jnp.sum(p_all, axis=1, keepdims=True) + p_new
    lam = _diff_lambda(lamv_ref[...], lam0)
    row = lax.broadcasted_iota(jnp.int32, (N_MAPS, 1), 0)
    coef = jnp.where(row < DIFF_HEADS, 1.0 / l, -lam / l)
    w_all = p_all * coef
    w_heads = w_all[:DIFF_HEADS, :] + w_all[DIFF_HEADS:, :]
    w_new = (p_new * coef)[:DIFF_HEADS, :] + (p_new * coef)[DIFF_HEADS:, :]
    tok_of_row = tok_ref[...]
    acc = w_new * vn_ref[0]
    for p in range(n_pages):
        w_p = w_heads[:, p * page:(p + 1) * page]
        w_rows = jnp.broadcast_to(w_p[None], (page, DIFF_HEADS, page)).reshape(page * DIFF_HEADS, page)
        w_col = jnp.sum(w_rows * tok_of_row, axis=1, keepdims=True)
        acc = acc + jnp.sum((w_col * v_refs[p][...]).reshape(page, DIFF_HEADS, DIFF_DV), axis=0)
    o_ref[0] = _rmsnorm(acc, sub_ref[...]) * (1.0 - lam0)


def diff_attention_step(q, k_new, v_new, cache_kt, cache_v2, layer, page_table, bias_tables, lamv, subln, lam0):
    nb = q.shape[0]
    n_pages = page_table.shape[1]
    page = cache_kt.shape[3]
    n_past = n_pages * page
    past_bias, new_bias = bias_tables
    sel, tok_of_row = _step_constants(page)
    tok_spec = pl.BlockSpec((1, 1, D_MODEL), lambda b, pt: (b, 0, 0))
    head_spec = pl.BlockSpec((1, DIFF_HEADS, DIFF_DV), lambda b, pt: (b, 0, 0))
    kt_spec = lambda p: pl.BlockSpec((None, None, D_MODEL, page),
                                     lambda b, pt: (layer, pt[b * n_pages + p], 0, 0))
    v_spec = lambda p: pl.BlockSpec((None, None, page * DIFF_HEADS, DIFF_DV),
                                    lambda b, pt: (layer, pt[b * n_pages + p], 0, 0))
    full = lambda a: pl.BlockSpec(a.shape, lambda b, pt: (0,) * a.ndim)
    sub2 = subln.reshape(1, DIFF_DV)
    grid_spec = pltpu.PrefetchScalarGridSpec(
        num_scalar_prefetch=1,
        grid=(nb,),
        in_specs=([tok_spec, tok_spec, head_spec]
                  + [kt_spec(p) for p in range(n_pages)] + [v_spec(p) for p in range(n_pages)]
                  + [full(past_bias), full(new_bias), full(sel), full(tok_of_row), full(lamv), full(sub2)]),
        out_specs=head_spec,
        scratch_shapes=[pltpu.VMEM((N_MAPS, n_past), F32)],
    )
    out = pl.pallas_call(
        functools.partial(_attn_step_kernel, n_pages=n_pages, page=page, lam0=lam0),
        out_shape=jax.ShapeDtypeStruct((nb, DIFF_HEADS, DIFF_DV), F32),
        grid_spec=grid_spec,
        compiler_params=_params("parallel"),
        name="diff_attention_step",
    )(page_table.reshape(-1), q.reshape(nb, 1, D_MODEL), k_new.reshape(nb, 1, D_MODEL),
      v_new.reshape(nb, DIFF_HEADS, DIFF_DV),
      *([cache_kt] * n_pages), *([cache_v2] * n_pages), past_bias, new_bias, sel, tok_of_row, lamv, sub2)
    return out.reshape(nb, D_MODEL)


ROW_CHUNKS = D_MODEL // LANES


def _router_kernel(*refs, tile_starts):
    n_groups = len(tile_starts)
    x_refs = refs[:n_groups]
    g_ref, w_ref, b_ref, h_ref, r_ref = refs[n_groups:]
    x = x_refs[0][...]
    for grp in range(1, n_groups):
        x = jnp.where(pl.program_id(0) >= tile_starts[grp], x_refs[grp][...], x)
    h = _rmsnorm(x, g_ref[...])
    tm = h.shape[0]
    for c in range(ROW_CHUNKS):
        h_ref[pl.ds(c, tm, stride=ROW_CHUNKS), :] = h[:, c * LANES:(c + 1) * LANES]
    h_hi, h_lo = _split_bf16(h)
    w_hi, w_lo = _split_bf16(w_ref[...])
    logits = _dot(h_hi, w_hi) + _dot(h_hi, w_lo) + _dot(h_lo, w_hi) + b_ref[...]
    lane = lax.broadcasted_iota(jnp.int32, logits.shape, 1)
    neg = jnp.float32(-jnp.inf)
    big = jnp.int32(2 ** 30)

    def first_argmax(vals):
        mx = jnp.max(vals, axis=1, keepdims=True)
        idx = jnp.min(jnp.where(vals == mx, lane, big), axis=1, keepdims=True)
        return mx, idx

    lg = jnp.where(lane < N_GROUPS, logits, neg)
    g_max, grp = first_argmax(lg)
    w_grp = 1.0 / jnp.sum(jnp.exp(lg - g_max), axis=1, keepdims=True)
    e_lane = lane - N_GROUPS
    in_grp = (e_lane >= grp * EXPERTS_PER_GROUP) & (e_lane < (grp + 1) * EXPERTS_PER_GROUP)
    le = jnp.where(in_grp, logits, neg)
    m1, i1 = first_argmax(le)
    m2, i2 = first_argmax(jnp.where(lane == i1, neg, le))
    e2 = jnp.exp(m2 - m1)
    gate1 = w_grp / (1.0 + e2)
    gate2 = w_grp * e2 / (1.0 + e2)
    f = lambda a: a.astype(F32)
    r_ref[...] = jnp.where(lane == 0, f(i1 - N_GROUPS),
                           jnp.where(lane == 1, f(i2 - N_GROUPS),
                                     jnp.where(lane == 2, gate1, jnp.where(lane == 3, gate2, 0.0))))


def moe_router(xs, g, w_route, b_route, tm=512):
    d = xs[0].shape[1]
    xs = [x if x.shape[0] % tm == 0 else jnp.pad(x, ((0, -x.shape[0] % tm), (0, 0))) for x in xs]
    n_tiles = [x.shape[0] // tm for x in xs]
    tile_starts = tuple(np.cumsum([0] + n_tiles[:-1]).tolist())
    total = sum(n_tiles) * tm

    def group_spec(start, n):
        return pl.BlockSpec((tm, d), lambda i: (jnp.clip(i - start, 0, n - 1), 0))

    h, route = pl.pallas_call(
        functools.partial(_router_kernel, tile_starts=tile_starts),
        out_shape=(jax.ShapeDtypeStruct((total * ROW_CHUNKS, LANES), F32),
                   jax.ShapeDtypeStruct((total, ROUTE_LANES), F32)),
        grid=(sum(n_tiles),),
        in_specs=[group_spec(s, n) for s, n in zip(tile_starts, n_tiles)]
                 + [pl.BlockSpec((1, d), lambda i: (0, 0)),
                    pl.BlockSpec((d, ROUTE_LANES), lambda i: (0, 0)),
                    pl.BlockSpec((1, ROUTE_LANES), lambda i: (0, 0))],
        out_specs=(pl.BlockSpec((tm * ROW_CHUNKS, LANES), lambda i: (i, 0)),
                   pl.BlockSpec((tm, ROUTE_LANES), lambda i: (i, 0))),
        compiler_params=_params("parallel"),
        name="moe_router",
    )(*xs, g.reshape(1, d), w_route, b_route)
    return h, route, [s * tm for s in tile_starts]


GATHER_GROUP = 8


def _expert_kernel(be_ref, nu_ref, nv_ref, tok_cur_ref, tok_next_ref, h_hbm, wg_ref, wu_ref, wd_ref, o_ref,
                   x_buf, x_sem, wg_s, wu_s, wd_s, *, tb):
    b = pl.program_id(0)
    n_blk = pl.num_programs(0)
    slot = lax.rem(b, 2)

    def start_row(tok_ref, dst_slot, r, priority):
        src_row = pl.multiple_of(tok_ref[0, r] * ROW_CHUNKS, ROW_CHUNKS)
        dst_row = pl.multiple_of(r * ROW_CHUNKS, ROW_CHUNKS)
        pltpu.make_async_copy(h_hbm.at[pl.ds(src_row, ROW_CHUNKS), :],
                              x_buf.at[dst_slot, pl.ds(dst_row, ROW_CHUNKS), :],
                              x_sem.at[dst_slot]).start(priority=priority)

    def start_gather(tok_ref, dst_slot, n_rows):
        def group(gi, carry):
            for j in range(GATHER_GROUP):
                start_row(tok_ref, dst_slot, gi * GATHER_GROUP + j, j % 2)
            return carry
        lax.fori_loop(0, n_rows // GATHER_GROUP, group, 0)

    def wait_gather(dst_slot, n_rows):
        @pl.when(n_rows > 0)
        def _():
            n = pl.multiple_of(n_rows * ROW_CHUNKS, ROW_CHUNKS)
            pltpu.make_async_copy(h_hbm.at[pl.ds(0, n), :], x_buf.at[dst_slot, pl.ds(0, n), :],
                                  x_sem.at[dst_slot]).wait()

    def load_block(src_slot):
        return jnp.concatenate([x_buf[src_slot, pl.ds(c, tb, stride=ROW_CHUNKS), :] for c in range(ROW_CHUNKS)],
                               axis=1).astype(BF16)

    @pl.when(b == 0)
    def _first():
        x_buf[...] = jnp.zeros_like(x_buf)
        start_gather(tok_cur_ref, 0, nv_ref[0])

    @pl.when(b + 1 < n_blk)
    def _prefetch():
        start_gather(tok_next_ref, 1 - slot, nv_ref[jnp.minimum(b + 1, n_blk - 1)])

    wait_gather(slot, nv_ref[b])
    prev = be_ref[jnp.maximum(b - 1, 0)]

    @pl.when((b == 0) | (be_ref[b] != prev))
    def _load_expert():
        wg_s[...] = wg_ref[...].astype(BF16)
        wu_s[...] = wu_ref[...].astype(BF16)
        wd_s[...] = wd_ref[...].astype(BF16)

    @pl.when(b < nu_ref[0])
    def _compute():
        x = load_block(slot)
        mid = _silu(_dot(x, wg_s[...])) * _dot(x, wu_s[...])
        o_ref[...] = _dot(mid.astype(BF16), wd_s[...])

    @pl.when(b >= nu_ref[0])
    def _unused():
        o_ref[...] = jnp.zeros_like(o_ref)


def moe_experts(h, row_tok, blk_e, n_used, blk_fill, w_gate, w_up, w_down, layer, tb=MOE_TILE):
    d = D_MODEL
    n_blk = row_tok.shape[0] // tb
    tok3 = row_tok.reshape(n_blk, 1, tb)
    w_in_spec = pl.BlockSpec((None, None, d, D_FF), lambda b, be, nu, nv: (layer, be[b], 0, 0))
    tok_spec = lambda nxt: pl.BlockSpec((None, 1, tb),
                                        lambda b, be, nu, nv: (jnp.minimum(b + nxt, n_blk - 1), 0, 0),
                                        memory_space=pltpu.SMEM)
    grid_spec = pltpu.PrefetchScalarGridSpec(
        num_scalar_prefetch=3,
        grid=(n_blk,),
        in_specs=[tok_spec(0), tok_spec(1),
                  pl.BlockSpec(memory_space=pl.ANY),
                  w_in_spec, w_in_spec,
                  pl.BlockSpec((None, None, D_FF, d), lambda b, be, nu, nv: (layer, be[b], 0, 0))],
        out_specs=pl.BlockSpec((tb, d), lambda b, be, nu, nv: (b, 0)),
        scratch_shapes=[pltpu.VMEM((2, tb * ROW_CHUNKS, LANES), F32), pltpu.SemaphoreType.DMA((2,)),
                        pltpu.VMEM((d, D_FF), BF16), pltpu.VMEM((d, D_FF), BF16), pltpu.VMEM((D_FF, d), BF16)],
    )
    return pl.pallas_call(
        functools.partial(_expert_kernel, tb=tb),
        out_shape=jax.ShapeDtypeStruct((n_blk * tb, d), F32),
        grid_spec=grid_spec,
        compiler_params=_params("arbitrary"),
        name="moe_experts",
    )(blk_e, n_used, blk_fill, tok3, tok3, h, w_gate, w_up, w_down)


def _dispatch_plan(experts, tb):
    t, k_sel = experts.shape
    n_asg = t * k_sel
    flat_e = experts.reshape(n_asg)
    onehot = (flat_e[:, None] == jnp.arange(N_EXPERTS, dtype=jnp.int32)[None, :]).astype(jnp.int32)
    csum = jnp.cumsum(onehot, axis=0)
    counts = csum[-1]
    starts = jnp.cumsum(counts) - counts
    padded = (counts + tb - 1) // tb * tb
    pad_end = jnp.cumsum(padded)
    pad_start = pad_end - padded
    dest = jnp.sum(onehot * (csum - 1 + pad_start[None, :]), axis=1)
    n_blk = (n_asg + N_EXPERTS * (tb - 1) + tb - 1) // tb
    blk_row0 = jnp.arange(n_blk, dtype=jnp.int32) * tb
    blk_e = jnp.minimum(jnp.sum((blk_row0[:, None] >= pad_end[None, :]).astype(jnp.int32), axis=1),
                        N_EXPERTS - 1)
    order = jnp.argsort(flat_e).astype(jnp.int32)
    within = (blk_row0 - pad_start[blk_e])[:, None] + jnp.arange(tb, dtype=jnp.int32)[None, :]
    src = jnp.clip(starts[blk_e][:, None] + within, 0, n_asg - 1)
    row_tok = jnp.where(within < counts[blk_e][:, None], order[src.reshape(-1)].reshape(n_blk, tb) // k_sel, 0)
    n_used = (pad_end[-1:] // tb).astype(jnp.int32)
    filled = jnp.clip(counts[blk_e] - (blk_row0 - pad_start[blk_e]), 0, tb)
    blk_fill = ((filled + GATHER_GROUP - 1) // GATHER_GROUP * GATHER_GROUP).astype(jnp.int32)
    return row_tok.reshape(-1), dest.reshape(t, k_sel), blk_e.astype(jnp.int32), n_used, blk_fill


def _combine_ple_kernel(x_ref, y0_ref, y1_ref, r_ref, p_ref, g_ref, wg_ref, wp_ref, o_ref):
    route = r_ref[...]
    x = x_ref[...] + route[:, 2:3] * y0_ref[...] + route[:, 3:4] * y1_ref[...]
    h = _rmsnorm(x, g_ref[...]).astype(BF16)
    gate = _sigmoid(_dot(h, wg_ref[...]))
    proj = _dot(p_ref[...].astype(BF16), wp_ref[...])
    o_ref[...] = x + gate * proj


def moe_combine_ple(x, y0, y1, route, row_offset, p, g, wg_bf16, wp_bf16, tm=512):
    t, d = x.shape
    tm = _row_tile(t, tm)
    assert row_offset % tm == 0
    off = row_offset // tm
    pd = p.shape[1]
    row = pl.BlockSpec((tm, d), lambda i: (i, 0))
    row_off = pl.BlockSpec((tm, d), lambda i: (i + off, 0))
    return pl.pallas_call(
        _combine_ple_kernel,
        out_shape=jax.ShapeDtypeStruct((t, d), F32),
        grid=(t // tm,),
        in_specs=[row, row_off, row_off,
                  pl.BlockSpec((tm, ROUTE_LANES), lambda i: (i + off, 0)),
                  pl.BlockSpec((tm, pd), lambda i: (i, 0)),
                  pl.BlockSpec((1, d), lambda i: (0, 0)),
                  pl.BlockSpec((d, d), lambda i: (0, 0)),
                  pl.BlockSpec((pd, d), lambda i: (0, 0))],
        out_specs=row,
        compiler_params=_params("parallel"),
        name="moe_combine_ple",
    )(x, y0, y1, route, p, g.reshape(1, d), wg_bf16, wp_bf16)


def moe_ple_add(xs, ps, layer, norm_g, rg_w, rg_b, re_w, re_b, w_gate, w_up, w_down, ple_g, wg_bf16, wp_bf16,
                tb=MOE_TILE):
    d = xs[0].shape[1]
    w_route = jnp.zeros((d, ROUTE_LANES), F32).at[:, :N_GROUPS].set(rg_w).at[:, N_GROUPS:N_GROUPS + N_EXPERTS].set(re_w)
    b_route = jnp.zeros((1, ROUTE_LANES), F32).at[0, :N_GROUPS].set(rg_b).at[0, N_GROUPS:N_GROUPS + N_EXPERTS].set(re_b)
    h, route, offsets = moe_router(xs, norm_g, w_route, b_route)
    row_tok, dest, blk_e, n_used, blk_fill = _dispatch_plan(route[:, 0:2].astype(jnp.int32), tb)
    yb = moe_experts(h, row_tok, blk_e, n_used, blk_fill, w_gate, w_up, w_down, layer, tb)
    y0, y1 = yb[dest[:, 0]], yb[dest[:, 1]]
    return [moe_combine_ple(x, y0, y1, route, off, p, ple_g, wg_bf16, wp_bf16)
            for x, p, off in zip(xs, ps, offsets)]


def _lambda_init(layer):
    return 0.8 - 0.6 * math.exp(-0.3 * layer)


def _permute_even_in(w_in):
    d = w_in.shape[0]
    lr0 = 2 * GLA_KEY_W + 2 * GLA_VAL_W
    return jnp.concatenate([w_in[:, :lr0], w_in[:, lr0 + GLA_RANK:], w_in[:, lr0:lr0 + GLA_RANK],
                            jnp.zeros((d, LANES - GLA_RANK), w_in.dtype)], axis=1).astype(BF16)


def kernel(x_prompt, x_sample, cache_k, cache_v, state_gla, state_conv, page_table, p_prompt, p_sample, rel_bias, norm_mix, norm_ffn, norm_ple, w_in_ev, w_lr_up, b_lr, gla_norm, conv_w, conv_b, conv_ln_g, conv_ln_b, w_out_ev, w_qkv, q_norm, k_norm, lam_q1, lam_k1, lam_q2, lam_k2, subln, w_out_od, router_g_w, router_g_b, router_e_w, router_e_b, moe_w_gate, moe_w_up, moe_w_down, ple_proj, ple_gate):
    n_bp, seq, d = x_prompt.shape
    n_bs = x_sample.shape[0]
    depth = norm_mix.shape[0]
    n_layers_odd, n_pool, page = cache_k.shape[:3]
    xp = x_prompt.reshape(n_bp * seq, d)
    xs = x_sample.reshape(n_bs, d)
    ck = jnp.transpose(cache_k, (0, 1, 3, 4, 5, 2)).reshape(n_layers_odd, n_pool, d, page)
    cv = cache_v.reshape(n_layers_odd, n_pool, page * DIFF_HEADS, DIFF_DV)
    prompt_bias = prompt_bias_tables(rel_bias)
    step_bias = step_bias_tables(rel_bias, page_table.shape[1] * page)
    step_rows = 16
    kp_l, vp_l, ks_l, vs_l, gp_l, gs_l, cp_l, cs_l = [], [], [], [], [], [], [], []
    for i in range(depth):
        j = i // 2
        if i % 2 == 0:
            w_in = _permute_even_in(w_in_ev[j])
            w_lr_pad = jnp.zeros((LANES, GLA_KEY_W), BF16).at[:GLA_RANK].set(w_lr_up[j].astype(BF16))
            cw_pad = jnp.zeros((CONV_HALO, CONV_CH), F32).at[:CONV_WIDTH].set(conv_w[j])
            w_out = w_out_ev[j].astype(BF16)
            conv_args = (cw_pad, conv_b[j], conv_ln_g[j], conv_ln_b[j])
            zp = norm_proj(xp, norm_mix[i], w_in)
            op, g_p = gla_mix(zp, n_bp, seq, GLA_CHUNK, 512, w_lr_pad, b_lr[j], gla_norm[j], None)
            cp, c_p = conv_mix(zp, n_bp, seq, *conv_args)
            xp = proj_residual([op, cp], w_out, xp)
            zs = norm_proj(xs, norm_mix[i], w_in)
            zs_pad = jnp.zeros((n_bs, step_rows, EVEN_Z_W), F32).at[:, 0].set(zs).reshape(n_bs * step_rows, EVEN_Z_W)
            os_pad, g_s = gla_mix(zs_pad, n_bs, step_rows, step_rows, 8 * step_rows, w_lr_pad, b_lr[j], gla_norm[j],
                                  state_gla[j].reshape(n_bs, GLA_KEY_W, GLA_DV), valid_rows=1)
            os_ = os_pad.reshape(n_bs, step_rows, GLA_VAL_W)[:, 0]
            cs, c_s = conv_step(zs, state_conv[j], *conv_args)
            xs = proj_residual([os_, cs], w_out, xs)
            gp_l.append(g_p.reshape(n_bp, GLA_HEADS, GLA_DK, GLA_DV))
            gs_l.append(g_s.reshape(n_bs, GLA_HEADS, GLA_DK, GLA_DV))
            cp_l.append(c_p)
            cs_l.append(c_s)
        else:
            lam0 = _lambda_init(i)
            lamv = jnp.stack([lam_q1[j], lam_k1[j], lam_q2[j], lam_k2[j]]).astype(F32)
            wq = w_qkv[j].astype(BF16)
            w_out = w_out_od[j].astype(BF16)
            qp, kp, vp, kpb, vpb = qkv_proj(xp, norm_mix[i], wq, q_norm[j], k_norm[j])
            ap = diff_attention(qp, kpb, vpb, n_bp, seq, prompt_bias, lamv, subln[j], lam0)
            xp = proj_residual([ap], w_out, xp)
            qs, ks, vs, _, _ = qkv_proj(xs, norm_mix[i], wq, q_norm[j], k_norm[j])
            as_ = diff_attention_step(qs, ks, vs, ck, cv, j, page_table, step_bias, lamv, subln[j], lam0)
            xs = proj_residual([as_], w_out, xs)
            kp_l.append(kp.reshape(n_bp, seq, DIFF_HEADS, 2, DIFF_DH))
            vp_l.append(vp.reshape(n_bp, seq, DIFF_HEADS, DIFF_DV))
            ks_l.append(ks.reshape(n_bs, 1, DIFF_HEADS, 2, DIFF_DH))
            vs_l.append(vs.reshape(n_bs, 1, DIFF_HEADS, DIFF_DV))
        tail_args = (i, norm_ffn[i], router_g_w[i], router_g_b[i], router_e_w[i], router_e_b[i],
                     moe_w_gate, moe_w_up, moe_w_down, norm_ple[i], ple_gate[i].astype(BF16), ple_proj[i].astype(BF16))
        xp, xs = moe_ple_add([xp, xs], [p_prompt[i].reshape(n_bp * seq, PLE_DIM), p_sample[i].reshape(n_bs, PLE_DIM)],
                             *tail_args)
    return (xp.reshape(n_bp, seq, d), xs.reshape(n_bs, 1, d),
            jnp.stack(kp_l), jnp.stack(vp_l), jnp.stack(ks_l), jnp.stack(vs_l),
            jnp.stack(gp_l), jnp.stack(gs_l), jnp.stack(cp_l), jnp.stack(cs_l))
```
